```python
import math
import jax, jax.numpy as jnp
from jax import lax
import numpy as np

D_MODEL = 2048
BATCH = 4
SEQ = 2048
DEPTH = 1
DEC_BATCH = 128
DEC_SEQ = 1
PAST_LEN = 2048
PAGE_SIZE = 128

A_HEADS = 8
A_DIM = 64
A_WIDTH = A_HEADS * 2 * A_DIM
S_HEADS = 16
S_HEAD_DIM = 64
S_INNER = S_HEADS * S_HEAD_DIM
S_GROUPS = 2
S_STATE = 128
S_CONV = 4
S_CHUNK = 128
S_CONV_DIM = S_INNER + 2 * S_GROUPS * S_STATE
MIX_WIDTH = A_WIDTH + S_INNER
SPLITS = (A_WIDTH, 2 * A_WIDTH, 3 * A_WIDTH, 3 * A_WIDTH + S_INNER, 3 * A_WIDTH + S_INNER + S_CONV_DIM)
IN_COLS = 3 * A_WIDTH + S_INNER + S_CONV_DIM + S_HEADS
N_MEM = 256
X_HEADS = 4
X_DIM = 128
X_WIDTH = X_HEADS * X_DIM
P_HEADS = 8
P_NKEYS = 128
P_EXPERTS = P_NKEYS * P_NKEYS
P_DKEY = 128
P_TOPK = 16
P_BLOCK = 128
Q_BLOCK = 128
EPS = 1e-6
F32 = jnp.float32

kernel_name = 'hymba_diffattn_ssd_peer_decode_step'


def rmsnorm(x, g):
    xf = x.astype(F32)
    y = xf * lax.rsqrt(jnp.mean(xf * xf, axis=-1, keepdims=True) + EPS)
    return (y * g.astype(F32)).astype(x.dtype)


def lambda_init(layer_idx):
    return 0.8 - 0.6 * math.exp(-0.3 * layer_idx)


def diff_weights(s, lam):
    p = jax.nn.softmax(s, axis=-1)
    return p[:, :, 0] - lam * p[:, :, 1]


def causal_conv(u, prev, w, bias):
    L = u.shape[1]
    up = jnp.concatenate([prev.astype(u.dtype), u], axis=1)
    out = bias
    for j in range(S_CONV):
        out = out + w[j] * up[:, j:j + L]
    return out, up[:, L:]


def segsum(a):
    T = a.shape[-1]
    ar = jnp.broadcast_to(a[..., :, None], a.shape + (T,))
    ar = jnp.where(jnp.tril(jnp.ones((T, T), bool), -1), ar, 0.0)
    ss = jnp.cumsum(ar, axis=-2)
    return jnp.where(jnp.tril(jnp.ones((T, T), bool)), ss, -jnp.inf)


def ssd_scan(x, dt, A, B, C, h0):
    b, L, H, P = x.shape
    n_chunks = -(-L // S_CHUNK)
    pad = n_chunks * S_CHUNK - L
    padf = lambda t: jnp.pad(t, [(0, 0), (0, pad)] + [(0, 0)] * (t.ndim - 2))
    x, dt, B, C = padf(x), padf(dt), padf(B), padf(C)
    ch = lambda t: t.reshape((b, n_chunks, S_CHUNK) + t.shape[2:])
    X = ch(x * dt[..., None].astype(x.dtype))
    Bc, Cc = ch(B), ch(C)
    a = jnp.moveaxis(ch(dt * A), -1, 1)
    a_cum = jnp.cumsum(a, axis=-1)
    Lmat = jnp.exp(segsum(a))
    y_diag = jnp.einsum('bclhn,bcshn,bhcls,bcshp->bclhp', Cc, Bc, Lmat, X)
    decay_states = jnp.exp(a_cum[..., -1:] - a_cum)
    states = jnp.einsum('bclhn,bhcl,bclhp->bchpn', Bc, decay_states, X)
    states = jnp.concatenate([h0[:, None].astype(states.dtype), states], axis=1)
    chunk_decay = jnp.exp(segsum(jnp.pad(a_cum[..., -1], [(0, 0), (0, 0), (1, 0)])))
    states = jnp.einsum('bhzc,bchpn->bzhpn', chunk_decay, states)
    y_off = jnp.einsum('bclhn,bchpn,bhcl->bclhp', Cc, states[:, :-1], jnp.exp(a_cum))
    y = (y_diag + y_off).reshape(b, n_chunks * S_CHUNK, H, P)[:, :L]
    return y, states[:, -1]


def memory_kv(mem, lw):
    b, m_len, _ = mem.shape
    m = rmsnorm(mem, lw['norm_mem'])
    mk = rmsnorm((m @ lw['wk_x']).reshape(b, m_len, X_HEADS, X_DIM), lw['kx_norm'])
    mv = (m @ lw['wv_x']).reshape(b, m_len, X_HEADS, X_DIM)
    return mk, mv


def peer(t, wq, keys, u_tab, v_tab):
    T = t.shape[0]
    nb = -(-T // P_BLOCK)
    tp = jnp.pad(t, [(0, nb * P_BLOCK - T), (0, 0)]).reshape(nb, P_BLOCK, D_MODEL)

    def block(tb):
        q = (tb @ wq).reshape(P_BLOCK, P_HEADS, 2, P_DKEY // 2)
        s = jnp.einsum('thcd,chkd->thck', q, keys.astype(q.dtype)).astype(F32)
        sv, si = lax.top_k(s, P_TOPK)
        cand = sv[..., 0, :, None] + sv[..., 1, None, :]
        cid = si[..., 0, :, None] * P_NKEYS + si[..., 1, None, :]
        fs, fi = lax.top_k(cand.reshape(P_BLOCK, P_HEADS, P_TOPK * P_TOPK), P_TOPK)
        eid = jnp.take_along_axis(cid.reshape(P_BLOCK, P_HEADS, P_TOPK * P_TOPK), fi, axis=-1)
        g = jax.nn.softmax(fs, axis=-1)
        act = jax.nn.gelu(jnp.einsum('td,thkd->thk', tb, u_tab[eid]).astype(F32), approximate=False)
        w = (g * act).astype(tb.dtype)
        return jnp.einsum('thk,thkd->td', w, v_tab[eid])

    return lax.map(block, tp).reshape(nb * P_BLOCK, D_MODEL)[:T]


def decoder_layer(x, layer_idx, lw, mem_k, mem_v, conv_prev, ssm_prev, past_k, past_v):
    b, L, _ = x.shape
    h = rmsnorm(x, lw['norm_mix'])
    proj = h @ lw['w_in']
    q, k, v, z, xbc, dt = jnp.split(proj, SPLITS, axis=-1)
    q = rmsnorm(q.reshape(b, L, A_HEADS, 2, A_DIM), lw['q_norm'])
    k = rmsnorm(k.reshape(b, L, A_HEADS, 2, A_DIM), lw['k_norm'])
    v = v.reshape(b, L, A_HEADS, 2 * A_DIM)
    lam0 = lambda_init(layer_idx)
    lam = (jnp.exp(jnp.sum(lw['lambda_q1'].astype(F32) * lw['lambda_k1'].astype(F32)))
           - jnp.exp(jnp.sum(lw['lambda_q2'].astype(F32) * lw['lambda_k2'].astype(F32))) + lam0)
    scale = A_DIM ** -0.5
    if past_k is None:
        nb = L // Q_BLOCK
        qb = jnp.moveaxis(q.reshape(b, nb, Q_BLOCK, A_HEADS, 2, A_DIM), 1, 0)
        k_pos = jnp.arange(L)

        def attend_block(args):
            qi, start = args
            s = jnp.einsum('bqhmd,bkhmd->bhmqk', qi, k).astype(F32) * scale
            q_pos = start + jnp.arange(Q_BLOCK)
            s = jnp.where(k_pos[None, :] <= q_pos[:, None], s, -jnp.inf)
            a = diff_weights(s, lam).astype(v.dtype)
            return jnp.einsum('bhqk,bkhe->bqhe', a, v)

        o = lax.map(attend_block, (qb, jnp.arange(nb) * Q_BLOCK))
        o = jnp.moveaxis(o, 0, 1).reshape(b, L, A_HEADS, 2 * A_DIM)
    else:
        P = past_k.shape[1]
        kp = past_k.reshape(b, P, A_HEADS, 2, A_DIM).astype(q.dtype)
        s_past = jnp.einsum('bqhmd,bkhmd->bhmqk', q, kp).astype(F32) * scale
        s_new = jnp.einsum('bqhmd,bkhmd->bhmqk', q, k).astype(F32) * scale
        s_new = jnp.where(jnp.tril(jnp.ones((L, L), bool)), s_new, -jnp.inf)
        a = diff_weights(jnp.concatenate([s_past, s_new], axis=-1), lam).astype(v.dtype)
        o = (jnp.einsum('bhqk,bkhe->bqhe', a[..., :P], past_v.astype(v.dtype))
             + jnp.einsum('bhqk,bkhe->bqhe', a[..., P:], v))
    o = rmsnorm(o, lw['subln']) * (1.0 - lam0)
    xbc, conv_state = causal_conv(xbc, conv_prev, lw['conv_w'], lw['conv_b'])
    xbc = jax.nn.silu(xbc)
    xs, Bm, Cm = jnp.split(xbc, (S_INNER, S_INNER + S_GROUPS * S_STATE), axis=-1)
    xs = xs.reshape(b, L, S_HEADS, S_HEAD_DIM)
    rep = S_HEADS // S_GROUPS
    Bm = jnp.repeat(Bm.reshape(b, L, S_GROUPS, S_STATE), rep, axis=2)
    Cm = jnp.repeat(Cm.reshape(b, L, S_GROUPS, S_STATE), rep, axis=2)
    dtv = jax.nn.softplus(dt.astype(F32) + lw['dt_bias'].astype(F32))
    A = -jnp.exp(lw['a_log'].astype(F32))
    y, ssm_state = ssd_scan(xs, dtv, A, Bm, Cm, ssm_prev)
    y = y + lw['d_skip'][:, None] * xs
    y = rmsnorm(y.reshape(b, L, S_INNER) * jax.nn.silu(z), lw['ssm_norm'])
    mixed = jnp.concatenate([o.reshape(b, L, A_WIDTH), y.astype(o.dtype)], axis=-1)
    x = x + mixed @ lw['w_out']
    hx = rmsnorm(x, lw['norm_x'])
    qx = rmsnorm((hx @ lw['wq_x']).reshape(b, L, X_HEADS, X_DIM), lw['qx_norm'])
    sx = jnp.einsum('bqhd,bmhd->bhqm', qx, mem_k.astype(qx.dtype)).astype(F32) * (X_DIM ** -0.5)
    px = jax.nn.softmax(sx, axis=-1).astype(mem_v.dtype)
    ox = jnp.einsum('bhqm,bmhd->bqhd', px, mem_v).reshape(b, L, X_WIDTH)
    x = x + ox.astype(x.dtype) @ lw['wo_x']
    hf = rmsnorm(x, lw['norm_ffn'])
    x = x + peer(hf.reshape(b * L, D_MODEL), lw['peer_wq'], lw['peer_keys'], lw['peer_u'], lw['peer_v']).reshape(b, L, D_MODEL)
    k_rows = k.reshape(b, L, A_HEADS, 2 * A_DIM)
    return x, k_rows, v, conv_state, ssm_state


def setup_inputs(seed: int = 0) -> dict:
    key = jax.random.key(seed)
    ks = iter(jax.random.split(key, 48))
    nrm = lambda shape, scale=1.0: jax.random.normal(next(ks), shape, F32) * scale
    gain = lambda shape: 1.0 + 0.02 * jax.random.normal(next(ks), shape, F32)
    n_pages = PAST_LEN // PAGE_SIZE
    n_used = DEC_BATCH * n_pages
    n_pool = n_used + max(1, n_used // 4)
    inp = {}
    inp['x_prompt'] = nrm((BATCH, SEQ, D_MODEL))
    inp['x_sample'] = nrm((DEC_BATCH, DEC_SEQ, D_MODEL))
    inp['cache_k'] = nrm((DEPTH, n_pool, PAGE_SIZE, A_HEADS, 2 * A_DIM))
    inp['cache_v'] = nrm((DEPTH, n_pool, PAGE_SIZE, A_HEADS, 2 * A_DIM))
    inp['cache_mem_k'] = nrm((DEPTH, DEC_BATCH, N_MEM, X_HEADS, X_DIM))
    inp['cache_mem_v'] = nrm((DEPTH, DEC_BATCH, N_MEM, X_HEADS, X_DIM))
    inp['state_conv'] = nrm((DEPTH, DEC_BATCH, S_CONV - 1, S_CONV_DIM))
    inp['state_ssm'] = nrm((DEPTH, DEC_BATCH, S_HEADS, S_HEAD_DIM, S_STATE), 0.1)
    inp['page_table'] = jax.random.permutation(next(ks), n_pool)[:n_used].reshape(DEC_BATCH, n_pages).astype(jnp.int32)
    inp['mem_prompt'] = nrm((BATCH, N_MEM, D_MODEL))
    inp['norm_mix'] = gain((DEPTH, D_MODEL))
    inp['w_in'] = nrm((DEPTH, D_MODEL, IN_COLS), D_MODEL ** -0.5)
    inp['q_norm'] = gain((DEPTH, A_DIM))
    inp['k_norm'] = gain((DEPTH, A_DIM))
    inp['lambda_q1'] = nrm((DEPTH, A_DIM), 0.1)
    inp['lambda_k1'] = nrm((DEPTH, A_DIM), 0.1)
    inp['lambda_q2'] = nrm((DEPTH, A_DIM), 0.1)
    inp['lambda_k2'] = nrm((DEPTH, A_DIM), 0.1)
    inp['subln'] = gain((DEPTH, 2 * A_DIM))
    inp['conv_w'] = nrm((DEPTH, S_CONV, S_CONV_DIM), S_CONV ** -0.5)
    inp['conv_b'] = nrm((DEPTH, S_CONV_DIM), 0.02)
    dt0 = jnp.exp(jax.random.uniform(next(ks), (DEPTH, S_HEADS), F32, math.log(1e-3), math.log(1e-1)))
    inp['dt_bias'] = dt0 + jnp.log(-jnp.expm1(-dt0))
    inp['a_log'] = jnp.log(jax.random.uniform(next(ks), (DEPTH, S_HEADS), F32, 1.0, 16.0))
    inp['d_skip'] = gain((DEPTH, S_HEADS))
    inp['ssm_norm'] = gain((DEPTH, S_INNER))
    inp['w_out'] = nrm((DEPTH, MIX_WIDTH, D_MODEL), MIX_WIDTH ** -0.5)
    inp['norm_x'] = gain((DEPTH, D_MODEL))
    inp['norm_mem'] = gain((DEPTH, D_MODEL))
    inp['wq_x'] = nrm((DEPTH, D_MODEL, X_WIDTH), D_MODEL ** -0.5)
    inp['wk_x'] = nrm((DEPTH, D_MODEL, X_WIDTH), D_MODEL ** -0.5)
    inp['wv_x'] = nrm((DEPTH, D_MODEL, X_WIDTH), D_MODEL ** -0.5)
    inp['qx_norm'] = gain((DEPTH, X_DIM))
    inp['kx_norm'] = gain((DEPTH, X_DIM))
    inp['wo_x'] = nrm((DEPTH, X_WIDTH, D_MODEL), X_WIDTH ** -0.5)
    inp['norm_ffn'] = gain((DEPTH, D_MODEL))
    inp['peer_wq'] = nrm((DEPTH, D_MODEL, P_HEADS * P_DKEY), D_MODEL ** -0.5)
    inp['peer_keys'] = nrm((DEPTH, 2, P_HEADS, P_NKEYS, P_DKEY // 2), (P_DKEY // 2) ** -0.5)
    inp['peer_u'] = nrm((DEPTH, P_EXPERTS, D_MODEL), D_MODEL ** -0.5)
    inp['peer_v'] = nrm((DEPTH, P_EXPERTS, D_MODEL), P_HEADS ** -0.5)
    return inp


def reference(x_prompt, x_sample, cache_k, cache_v, cache_mem_k, cache_mem_v, state_conv, state_ssm,
              page_table, mem_prompt, norm_mix, w_in, q_norm, k_norm, lambda_q1, lambda_k1, lambda_q2,
              lambda_k2, subln, conv_w, conv_b, dt_bias, a_log, d_skip, ssm_norm, w_out, norm_x, norm_mem,
              wq_x, wk_x, wv_x, qx_norm, kx_norm, wo_x, norm_ffn, peer_wq, peer_keys, peer_u, peer_v):
    bp = x_prompt.shape[0]
    bs = x_sample.shape[0]
    yp, ys = x_prompt, x_sample
    kp_l, vp_l, cp_l, sp_l, mkp_l, mvp_l = [], [], [], [], [], []
    ks_l, vs_l, cs_l, ss_l = [], [], [], []
    for l in range(DEPTH):
        lw = dict(norm_mix=norm_mix[l], w_in=w_in[l], q_norm=q_norm[l], k_norm=k_norm[l],
                  lambda_q1=lambda_q1[l], lambda_k1=lambda_k1[l], lambda_q2=lambda_q2[l], lambda_k2=lambda_k2[l],
                  subln=subln[l], conv_w=conv_w[l], conv_b=conv_b[l], dt_bias=dt_bias[l], a_log=a_log[l],
                  d_skip=d_skip[l], ssm_norm=ssm_norm[l], w_out=w_out[l], norm_x=norm_x[l], norm_mem=norm_mem[l],
                  wq_x=wq_x[l], wk_x=wk_x[l], wv_x=wv_x[l], qx_norm=qx_norm[l], kx_norm=kx_norm[l], wo_x=wo_x[l],
                  norm_ffn=norm_ffn[l], peer_wq=peer_wq[l], peer_keys=peer_keys[l], peer_u=peer_u[l], peer_v=peer_v[l])
        mk, mv = memory_kv(mem_prompt, lw)
        conv0 = jnp.zeros((bp, S_CONV - 1, S_CONV_DIM), x_prompt.dtype)
        ssm0 = jnp.zeros((bp, S_HEADS, S_HEAD_DIM, S_STATE), F32)
        yp, kr, vr, cst, sst = decoder_layer(yp, l, lw, mk, mv, conv0, ssm0, None, None)
        kp_l.append(kr); vp_l.append(vr); cp_l.append(cst); sp_l.append(sst); mkp_l.append(mk); mvp_l.append(mv)
        past_k = cache_k[l][page_table].reshape(bs, -1, A_HEADS, 2 * A_DIM)
        past_v = cache_v[l][page_table].reshape(bs, -1, A_HEADS, 2 * A_DIM)
        ys, kr, vr, cst, sst = decoder_layer(ys, l, lw, cache_mem_k[l], cache_mem_v[l], state_conv[l], state_ssm[l], past_k, past_v)
        ks_l.append(kr); vs_l.append(vr); cs_l.append(cst); ss_l.append(sst)
    new_k_prompt = jnp.stack(kp_l)
    new_v_prompt = jnp.stack(vp_l)
    new_conv_prompt = jnp.stack(cp_l)
    new_ssm_prompt = jnp.stack(sp_l)
    new_mem_k_prompt = jnp.stack(mkp_l)
    new_mem_v_prompt = jnp.stack(mvp_l)
    new_k_sample = jnp.stack(ks_l)
    new_v_sample = jnp.stack(vs_l)
    new_conv_sample = jnp.stack(cs_l)
    new_ssm_sample = jnp.stack(ss_l)
    return (yp, ys, new_k_prompt, new_v_prompt, new_conv_prompt, new_ssm_prompt, new_mem_k_prompt,
            new_mem_v_prompt, new_k_sample, new_v_sample, new_conv_sample, new_ssm_sample)
```

```python
import functools
import math

import jax
import jax.numpy as jnp
from jax import lax
from jax.experimental import pallas as pl
from jax.experimental.pallas import tpu as pltpu

F32 = jnp.float32
BF16 = jnp.bfloat16
I32 = jnp.int32

EPS = 1e-6
LANES = 128
SUBLANES = 8
S_GROUPS = 2
S_CHUNK = 128
P_TOPK = 16
PAGES_PER_STEP = 4
VMEM_LIMIT = 56 << 20

NT_DIMS = (((1,), (1,)), ((), ()))


def _params(sem):
    return pltpu.CompilerParams(dimension_semantics=sem, vmem_limit_bytes=VMEM_LIMIT)


def _dot(a, b):
    return jnp.dot(a, b, preferred_element_type=F32)


def _dot_nt(a, b):
    return lax.dot_general(a, b, NT_DIMS, preferred_element_type=F32)


def _split3(x):
    hi = x.astype(BF16)
    r1 = x - hi.astype(F32)
    mid = r1.astype(BF16)
    lo = (r1 - mid.astype(F32)).astype(BF16)
    return hi, mid, lo


def _dot3(x, m_bf16):
    hi, mid, lo = _split3(x)
    return _dot(hi, m_bf16) + _dot(mid, m_bf16) + _dot(lo, m_bf16)


def _rms_rows(x, gain):
    ms = jnp.mean(x * x, axis=-1, keepdims=True)
    return x * lax.rsqrt(ms + EPS) * gain


def _head_rms(x, gain, width):
    outs = []
    for h in range(x.shape[1] // width):
        c = x[:, h * width:(h + 1) * width]
        outs.append(_rms_rows(c, gain))
    return outs[0] if len(outs) == 1 else jnp.concatenate(outs, axis=1)


def _silu(x):
    return x * (1.0 / (1.0 + jnp.exp(-x)))


def _softplus(x):
    return jnp.maximum(x, 0.0) + jnp.log1p(jnp.exp(-jnp.abs(x)))


def _inproj_kernel(x_ref, g_ref, w_ref, cg_ref, bd_ref, q_ref, k_ref, v_ref, z_ref, xd_ref,
                   xn_ref, *, group):
    n = pl.program_id(1)

    @pl.when(n == 0)
    def _():
        xn_ref[...] = _rms_rows(x_ref[...], g_ref[...]).astype(BF16)

    acc = _dot(xn_ref[...], w_ref[...])
    tn = acc.shape[1]

    def group_norm(dst_ref):
        for c in range(tn // LANES):
            a = acc[:, c * LANES:(c + 1) * LANES]
            sq = a * a
            hi = sq.astype(BF16)
            lo = (sq - hi.astype(F32)).astype(BF16)
            ss = _dot(hi, bd_ref[...]) + _dot(lo, bd_ref[...])
            y = a * lax.rsqrt(ss * (1.0 / group) + EPS) * cg_ref[:, c * LANES:(c + 1) * LANES]
            dst_ref[:, c * LANES:(c + 1) * LANES] = y.astype(dst_ref.dtype)

    @pl.when(n == 0)
    def _():
        group_norm(q_ref)

    @pl.when(n == 1)
    def _():
        group_norm(k_ref)

    @pl.when(n == 2)
    def _():
        v_ref[...] = acc

    @pl.when(n == 3)
    def _():
        z_ref[...] = acc

    @pl.when(n >= 4)
    def _():
        xd_ref[...] = acc


def _inproj(x, gain, w_pad, colgain, bd, a_dim, tm):
    t, d = x.shape
    n_pad = w_pad.shape[1]
    tn = colgain.shape[1] // 2
    nt = n_pad // tn
    nx = nt - 4
    grid = (t // tm, nt)
    row = lambda m, n: (m, 0)
    return pl.pallas_call(
        functools.partial(_inproj_kernel, group=a_dim),
        grid=grid,
        in_specs=[
            pl.BlockSpec((tm, d), row),
            pl.BlockSpec((1, d), lambda m, n: (0, 0)),
            pl.BlockSpec((d, tn), lambda m, n: (0, n)),
            pl.BlockSpec((1, tn), lambda m, n: (0, jnp.minimum(n, 1))),
            pl.BlockSpec((LANES, LANES), lambda m, n: (0, 0)),
        ],
        out_specs=[
            pl.BlockSpec((tm, tn), row),
            pl.BlockSpec((tm, tn), row),
            pl.BlockSpec((tm, tn), row),
            pl.BlockSpec((tm, tn), row),
            pl.BlockSpec((tm, tn), lambda m, n: (m, jnp.clip(n - 4, 0, nx - 1))),
        ],
        out_shape=[
            jax.ShapeDtypeStruct((t, tn), BF16),
            jax.ShapeDtypeStruct((t, tn), F32),
            jax.ShapeDtypeStruct((t, tn), F32),
            jax.ShapeDtypeStruct((t, tn), F32),
            jax.ShapeDtypeStruct((t, nx * tn), F32),
        ],
        scratch_shapes=[pltpu.VMEM((tm, d), BF16)],
        compiler_params=_params(("parallel", "arbitrary")),
        name="in_proj",
    )(x, gain, w_pad, colgain, bd)


def _lambda(lq1, lk1, lq2, lk2, lam0):
    return (jnp.exp(jnp.sum(lq1[...] * lk1[...], axis=-1, keepdims=True))
            - jnp.exp(jnp.sum(lq2[...] * lk2[...], axis=-1, keepdims=True)) + lam0)


def _attn_prompt_kernel(q_ref, k_ref, v_ref, sub_ref, lq1, lk1, lq2, lk2, o_ref,
                        m_ref, l_ref, acc_ref, *, a_dim, lam0):
    qi = pl.program_id(2)
    tq = q_ref.shape[0]
    q = q_ref[...]
    lane = lax.broadcasted_iota(I32, q.shape, 1)
    zero = jnp.zeros_like(q)
    q2 = jnp.concatenate([jnp.where(lane < a_dim, q, zero), jnp.where(lane >= a_dim, q, zero)], axis=0)

    m_ref[...] = jnp.full(m_ref.shape, -jnp.inf, F32)
    l_ref[...] = jnp.zeros(l_ref.shape, F32)
    acc_ref[...] = jnp.zeros(acc_ref.shape, F32)

    def step(j, masked):
        off = pl.multiple_of(j * tq, tq)
        kt = k_ref[pl.ds(off, tq), :].astype(BF16)
        vt = v_ref[pl.ds(off, tq), :].astype(BF16)
        s = _dot_nt(q2, kt)
        if masked:
            r = lax.broadcasted_iota(I32, s.shape, 0)
            c = lax.broadcasted_iota(I32, s.shape, 1)
            r = jnp.where(r >= tq, r - tq, r)
            s = jnp.where(c <= r, s, -jnp.inf)
        m_old = m_ref[...]
        m_new = jnp.maximum(m_old, jnp.max(s, axis=-1, keepdims=True))
        alpha = jnp.exp(m_old - m_new)
        p = jnp.exp(s - m_new)
        l_ref[...] = alpha * l_ref[...] + jnp.sum(p, axis=-1, keepdims=True)
        acc_ref[...] = alpha * acc_ref[...] + _dot(p.astype(BF16), vt)
        m_ref[...] = m_new

    def body(j, carry):
        step(j, False)
        return carry

    lax.fori_loop(0, qi, body, 0)
    step(qi, True)

    lam = _lambda(lq1, lk1, lq2, lk2, lam0)
    o1 = acc_ref[0:tq, :] / l_ref[0:tq, :]
    o2 = acc_ref[tq:2 * tq, :] / l_ref[tq:2 * tq, :]
    o = o1 - lam * o2
    o_ref[...] = (_rms_rows(o, sub_ref[...]) * (1.0 - lam0)).astype(o_ref.dtype)


def _attn_prompt(q, k, v, subln, lams, batch, seq, a_dim, lam0, tq):
    t, width = q.shape
    hd = 2 * a_dim
    heads = width // hd
    nq = seq // tq
    small = pl.BlockSpec((1, a_dim), lambda b, h, i: (0, 0))
    return pl.pallas_call(
        functools.partial(_attn_prompt_kernel, a_dim=a_dim, lam0=lam0),
        grid=(batch, heads, nq),
        in_specs=[
            pl.BlockSpec((tq, hd), lambda b, h, i: (b * nq + i, h)),
            pl.BlockSpec((seq, hd), lambda b, h, i: (b, h)),
            pl.BlockSpec((seq, hd), lambda b, h, i: (b, h)),
            pl.BlockSpec((1, hd), lambda b, h, i: (0, 0)),
            small, small, small, small,
        ],
        out_specs=pl.BlockSpec((tq, hd), lambda b, h, i: (b * nq + i, h)),
        out_shape=jax.ShapeDtypeStruct((t, width), BF16),
        scratch_shapes=[pltpu.VMEM((2 * tq, 1), F32), pltpu.VMEM((2 * tq, 1), F32),
                        pltpu.VMEM((2 * tq, hd), F32)],
        compiler_params=_params(("parallel", "parallel", "arbitrary")),
        name="diff_attn_prompt",
    )(q, k, v, subln, *lams)


def _attn_decode_kernel(pt_ref, q_ref, kn_ref, vn_ref, sub_ref, lq1, lk1, lq2, lk2, *rest,
                        a_dim, lam0, heads):
    del pt_ref
    kp = rest[:PAGES_PER_STEP]
    vp = rest[PAGES_PER_STEP:2 * PAGES_PER_STEP]
    o_ref, qr_ref, m_ref, l_ref, acc_ref = rest[2 * PAGES_PER_STEP:]
    p_id = pl.program_id(1)
    rows = 2 * heads
    width = q_ref.shape[-1]
    hd = 2 * a_dim

    @pl.when(p_id == 0)
    def _():
        qb = jnp.broadcast_to(q_ref[0], (rows, width))
        r = lax.broadcasted_iota(I32, (rows, width), 0)
        c = lax.broadcasted_iota(I32, (rows, width), 1)
        owner = ((c % hd) // a_dim) * heads + c // hd
        qr_ref[...] = jnp.where(r == owner, qb, 0.0).astype(BF16)
        m_ref[...] = jnp.full(m_ref.shape, -jnp.inf, F32)
        l_ref[...] = jnp.zeros(l_ref.shape, F32)
        acc_ref[...] = jnp.zeros(acc_ref.shape, F32)

    qr = qr_ref[...]

    def fold(s, pv_fn):
        m_old = m_ref[...]
        m_new = jnp.maximum(m_old, jnp.max(s, axis=-1, keepdims=True))
        alpha = jnp.exp(m_old - m_new)
        p = jnp.exp(s - m_new)
        l_ref[...] = alpha * l_ref[...] + jnp.sum(p, axis=-1, keepdims=True)
        acc_ref[...] = alpha * acc_ref[...] + pv_fn(p)
        m_ref[...] = m_new

    for i in range(PAGES_PER_STEP):
        kt = kp[i][0].astype(BF16)
        vt = vp[i][0].astype(BF16)
        fold(_dot_nt(qr, kt), lambda p, vt=vt: _dot(p.astype(BF16), vt))

    @pl.when(p_id == pl.num_programs(1) - 1)
    def _():
        kn = kn_ref[0].astype(BF16).astype(F32)
        s_new = jnp.sum(qr.astype(F32) * kn, axis=-1, keepdims=True)
        vn = vn_ref[0].astype(BF16).astype(F32)
        fold(s_new, lambda p: p.astype(BF16).astype(F32) * vn)

        lam = _lambda(lq1, lk1, lq2, lk2, lam0)
        o1 = acc_ref[0:heads, :] / l_ref[0:heads, :]
        o2 = acc_ref[heads:rows, :] / l_ref[heads:rows, :]
        o = o1 - lam * o2
        r = lax.broadcasted_iota(I32, o.shape, 0)
        c = lax.broadcasted_iota(I32, o.shape, 1)
        o = jnp.sum(jnp.where(r == c // hd, o, 0.0), axis=0, keepdims=True)
        o_ref[0] = _head_rms(o, sub_ref[...], hd) * (1.0 - lam0)


def _attn_decode(q, k_new, v_new, cache_k, cache_v, page_table, subln, lams, a_dim, lam0):
    bs, width = q.shape
    n_pool, page = cache_k.shape[0], cache_k.shape[1]
    n_pages = page_table.shape[1]
    heads = width // (2 * a_dim)
    steps = n_pages // PAGES_PER_STEP
    pt = page_table.reshape(-1)
    q3, k3, v3 = (a.astype(F32).reshape(bs, 1, width) for a in (q, k_new, v_new))
    row = pl.BlockSpec((1, 1, width), lambda b, p, pt: (b, 0, 0))
    small = pl.BlockSpec((1, a_dim), lambda b, p, pt: (0, 0))

    def page_spec(i):
        return pl.BlockSpec((1, page, width),
                            lambda b, p, pt: (pt[b * n_pages + p * PAGES_PER_STEP + i], 0, 0))

    grid_spec = pltpu.PrefetchScalarGridSpec(
        num_scalar_prefetch=1,
        grid=(bs, steps),
        in_specs=[row, row, row, pl.BlockSpec((1, 2 * a_dim), lambda b, p, pt: (0, 0)),
                  small, small, small, small]
        + [page_spec(i) for i in range(PAGES_PER_STEP)] * 2,
        out_specs=row,
        scratch_shapes=[pltpu.VMEM((2 * heads, width), BF16), pltpu.VMEM((2 * heads, 1), F32),
                        pltpu.VMEM((2 * heads, 1), F32), pltpu.VMEM((2 * heads, width), F32)],
    )
    out = pl.pallas_call(
        functools.partial(_attn_decode_kernel, a_dim=a_dim, lam0=lam0, heads=heads),
        grid_spec=grid_spec,
        out_shape=jax.ShapeDtypeStruct((bs, 1, width), F32),
        compiler_params=_params(("parallel", "arbitrary")),
        name="diff_attn_decode",
    )(pt, q3, k3, v3, subln, *lams, *([cache_k] * PAGES_PER_STEP), *([cache_v] * PAGES_PER_STEP))
    return out.reshape(bs, width)


def _conv_taps(cur, tail, w_ref, b_ref):
    n_tap = w_ref.shape[0]
    out = b_ref[...] + w_ref[n_tap - 1:n_tap, :] * cur
    row8 = lax.broadcasted_iota(I32, (SUBLANES, cur.shape[1]), 0)
    for s in range(1, n_tap):
        rolled = pltpu.roll(cur, s, 0)
        head = jnp.where(row8 < s, pltpu.roll(tail, s, 0), rolled[0:SUBLANES])
        shifted = jnp.concatenate([head, rolled[SUBLANES:]], axis=0)
        out = out + w_ref[n_tap - 1 - s:n_tap - s, :] * shifted
    return out


def _ssd_prompt_kernel(xd_ref, z_ref, cw_ref, cb_ref, dtb_ref, alog_ref, dsk_ref, ng_ref, tri_ref,
                       y_ref, st_ref, tail_ref, h_ref, yg_ref, *, inner, state, conv_dim, head_dim):
    c_id = pl.program_id(1)
    ch = xd_ref.shape[0]
    pairs = h_ref.shape[0]
    rep = (inner // head_dim) // S_GROUPS

    @pl.when(c_id == 0)
    def _():
        tail_ref[...] = jnp.zeros(tail_ref.shape, F32)
        h_ref[...] = jnp.zeros(h_ref.shape, F32)

    pre = xd_ref[:, 0:conv_dim]
    xbc = _silu(_conv_taps(pre, tail_ref[...], cw_ref, cb_ref))
    tail_ref[...] = pre[ch - SUBLANES:ch, :]

    dtv = _softplus(xd_ref[:, conv_dim:conv_dim + LANES] + dtb_ref[...])
    a = dtv * (-jnp.exp(alog_ref[...]))
    a_hi, a_mid, a_lo = _split3(a)
    tri = tri_ref[...]
    a_cum = _dot(tri, a_hi) + _dot(tri, a_mid) + _dot(tri, a_lo)
    a_cum_t = a_cum.T
    row = lax.broadcasted_iota(I32, (ch, ch), 0)
    col = lax.broadcasted_iota(I32, (ch, ch), 1)
    lane = lax.broadcasted_iota(I32, (ch, LANES), 1)
    first = lane < head_dim
    causal = col <= row

    ssq = jnp.zeros((ch, 1), F32)
    bms = [xbc[:, inner + g * state:inner + (g + 1) * state].astype(BF16) for g in range(S_GROUPS)]
    cms = [xbc[:, inner + (S_GROUPS + g) * state:inner + (S_GROUPS + g + 1) * state].astype(BF16)
           for g in range(S_GROUPS)]
    cb = [_dot_nt(cms[g], bms[g]) for g in range(S_GROUPS)]
    for pr in range(pairs):
        g = (2 * pr) // rep
        bm, cm = bms[g], cms[g]
        xs = xbc[:, pr * LANES:(pr + 1) * LANES]
        ha, hb = 2 * pr, 2 * pr + 1
        dt2 = jnp.where(first, dtv[:, ha:ha + 1], dtv[:, hb:hb + 1])
        xdt = xs * dt2
        ac2 = jnp.where(first, a_cum[:, ha:ha + 1], a_cum[:, hb:hb + 1])
        last2 = ac2[ch - 1:ch, :]
        y = jnp.zeros((ch, LANES), F32)
        for hh, keep in ((ha, first), (hb, jnp.logical_not(first))):
            seg = a_cum[:, hh:hh + 1] - a_cum_t[hh:hh + 1, :]
            lm = jnp.exp(jnp.where(causal, seg, -jnp.inf))
            y = y + _dot((cb[g] * lm).astype(BF16), jnp.where(keep, xdt, 0.0).astype(BF16))
        hprev = h_ref[pr]
        y = y + _dot_nt(cm, hprev.astype(BF16)) * jnp.exp(ac2)
        xdec = (xdt * jnp.exp(last2 - ac2)).T
        upd = _dot(xdec.astype(BF16), bm)
        rowp = lax.broadcasted_iota(I32, hprev.shape, 0)
        e_last = jnp.exp(last2)
        decay = jnp.where(rowp < head_dim, e_last[:, 0:1], e_last[:, head_dim:head_dim + 1])
        h_ref[pr] = decay * hprev + upd
        y = y + dsk_ref[:, pr * LANES:(pr + 1) * LANES] * xs
        y = y * _silu(z_ref[:, pr * LANES:(pr + 1) * LANES])
        yg_ref[:, pr * LANES:(pr + 1) * LANES] = y
        ssq = ssq + jnp.sum(y * y, axis=-1, keepdims=True)

    y_ref[...] = (yg_ref[...] * lax.rsqrt(ssq * (1.0 / inner) + EPS) * ng_ref[...]).astype(y_ref.dtype)

    @pl.when(c_id == pl.num_programs(1) - 1)
    def _():
        st_ref[0] = h_ref[...]


def _ssd_prompt(xd, z, conv_w, conv_b, dtb_pad, alog_pad, dskip_exp, norm_g, tri, batch, seq,
                inner, state, conv_dim, head_dim):
    t = xd.shape[0]
    nc = seq // S_CHUNK
    pairs = inner // LANES
    const = lambda shape: pl.BlockSpec(shape, lambda b, c: (0,) * len(shape))
    return pl.pallas_call(
        functools.partial(_ssd_prompt_kernel, inner=inner, state=state, conv_dim=conv_dim,
                          head_dim=head_dim),
        grid=(batch, nc),
        in_specs=[
            pl.BlockSpec((S_CHUNK, xd.shape[1]), lambda b, c: (b * nc + c, 0)),
            pl.BlockSpec((S_CHUNK, inner), lambda b, c: (b * nc + c, 0)),
            const(conv_w.shape), const(conv_b.shape), const(dtb_pad.shape), const(alog_pad.shape),
            const(dskip_exp.shape), const(norm_g.shape), const(tri.shape),
        ],
        out_specs=[
            pl.BlockSpec((S_CHUNK, inner), lambda b, c: (b * nc + c, 0)),
            pl.BlockSpec((1, pairs, LANES, state), lambda b, c: (b, 0, 0, 0)),
        ],
        out_shape=[jax.ShapeDtypeStruct((t, inner), BF16),
                   jax.ShapeDtypeStruct((batch, pairs, LANES, state), F32)],
        scratch_shapes=[pltpu.VMEM((SUBLANES, conv_dim), F32), pltpu.VMEM((pairs, LANES, state), F32),
                        pltpu.VMEM((S_CHUNK, inner), F32)],
        compiler_params=_params(("parallel", "arbitrary")),
        name="ssd_prompt",
    )(xd, z, conv_w, conv_b, dtb_pad, alog_pad, dskip_exp, norm_g, tri)


def _ssd_step_prep_kernel(xd_ref, cs_ref, cw_ref, cb_ref, dtb_ref, alog_ref, exp_ref,
                          xbc_ref, xdt_t_ref, da_ref, *, inner, conv_dim):
    n_tap = cw_ref.shape[0]
    out = cb_ref[...] + cw_ref[n_tap - 1:n_tap, :] * xd_ref[:, 0:conv_dim]
    for j in range(n_tap - 1):
        out = out + cw_ref[j:j + 1, :] * cs_ref[j]
    xbc = _silu(out)
    xbc_ref[...] = xbc
    dtv = _softplus(xd_ref[:, conv_dim:conv_dim + LANES] + dtb_ref[...])
    da_ref[...] = jnp.exp(dtv * (-jnp.exp(alog_ref[...])))
    dt_exp = _dot3(dtv, exp_ref[...])
    xdt_t_ref[...] = (xbc[:, 0:inner] * dt_exp).T


def _ssd_step_prep(xd, conv_state_t, conv_w, conv_b, dtb_pad, alog_pad, expand, inner, conv_dim):
    bs = xd.shape[0]
    return pl.pallas_call(
        functools.partial(_ssd_step_prep_kernel, inner=inner, conv_dim=conv_dim),
        out_shape=[jax.ShapeDtypeStruct((bs, conv_dim), F32),
                   jax.ShapeDtypeStruct((inner, bs), F32),
                   jax.ShapeDtypeStruct((bs, LANES), F32)],
        compiler_params=pltpu.CompilerParams(vmem_limit_bytes=VMEM_LIMIT),
        name="ssd_step_prep",
    )(xd, conv_state_t, conv_w, conv_b, dtb_pad, alog_pad, expand)


def _ssd_step_kernel(da_ref, h_ref, xdt_t_ref, bc_ref, hn_ref, yt_ref, *, state, head_dim):
    b = pl.program_id(0)
    bs = bc_ref.shape[0]
    rows = h_ref.shape[1]
    gr = rows // S_GROUPS

    @pl.when(b == 0)
    def _():
        yt_ref[...] = jnp.zeros(yt_ref.shape, F32)

    sel = lax.broadcasted_iota(I32, (bs, state), 0) == b
    for g in range(S_GROUPS):
        b_sel = jnp.where(sel, bc_ref[:, g * state:(g + 1) * state], 0.0)
        c_sel = jnp.where(sel, bc_ref[:, (S_GROUPS + g) * state:(S_GROUPS + g + 1) * state], 0.0)
        xt = xdt_t_ref[g * gr:(g + 1) * gr, :]
        x_hi = xt.astype(BF16)
        x_lo = (xt - x_hi.astype(F32)).astype(BF16)
        b_hi = b_sel.astype(BF16)
        b_lo = (b_sel - b_hi.astype(F32)).astype(BF16)
        upd = _dot(x_hi, b_hi) + _dot(x_lo, b_hi) + _dot(x_hi, b_lo)
        hn_g = []
        for hh in range(gr // head_dim):
            r0 = g * gr + hh * head_dim
            da = da_ref[b * (rows // head_dim) + r0 // head_dim]
            hn = da * h_ref[0, r0:r0 + head_dim, :] + upd[hh * head_dim:(hh + 1) * head_dim, :]
            hn_ref[0, r0:r0 + head_dim, :] = hn
            hn_g.append(hn.astype(BF16))
        hn_g = jnp.concatenate(hn_g, axis=0)
        yt_ref[g * gr:(g + 1) * gr, :] += _dot_nt(hn_g, c_sel.astype(BF16))


def _ssd_step(da, h, xdt_t, bc, state, head_dim):
    bs, rows, _ = h.shape
    return pl.pallas_call(
        functools.partial(_ssd_step_kernel, state=state, head_dim=head_dim),
        grid=(bs,),
        in_specs=[
            pl.BlockSpec(memory_space=pltpu.SMEM),
            pl.BlockSpec((1, rows, state), lambda b: (b, 0, 0)),
            pl.BlockSpec(xdt_t.shape, lambda b: (0, 0)),
            pl.BlockSpec(bc.shape, lambda b: (0, 0)),
        ],
        out_specs=[
            pl.BlockSpec((1, rows, state), lambda b: (b, 0, 0)),
            pl.BlockSpec((rows, bs), lambda b: (0, 0)),
        ],
        out_shape=[jax.ShapeDtypeStruct(h.shape, F32), jax.ShapeDtypeStruct((rows, bs), F32)],
        compiler_params=_params(("arbitrary",)),
        name="ssd_step",
    )(da, h, xdt_t, bc)


def _ssd_step_out_kernel(yt_ref, xbc_ref, z_ref, dsk_ref, ng_ref, y_ref, *, inner):
    y = yt_ref[...].T + dsk_ref[...] * xbc_ref[:, 0:inner]
    y = y * _silu(z_ref[...])
    y_ref[...] = _rms_rows(y, ng_ref[...]).astype(y_ref.dtype)


def _ssd_step_out(yt, xbc, z, dskip_exp, norm_g, inner):
    bs = xbc.shape[0]
    return pl.pallas_call(
        functools.partial(_ssd_step_out_kernel, inner=inner),
        out_shape=jax.ShapeDtypeStruct((bs, inner), BF16),
        compiler_params=pltpu.CompilerParams(vmem_limit_bytes=VMEM_LIMIT),
        name="ssd_step_out",
    )(yt, xbc, z, dskip_exp, norm_g)


def _outproj_kernel(x_ref, o_ref, y_ref, wo_ref, nx_ref, wq_ref, qn_ref, x1_ref, qx_ref, *, a_width, x_dim):
    x1 = x_ref[...] + _dot(o_ref[...], wo_ref[0:a_width, :]) + _dot(y_ref[...], wo_ref[a_width:, :])
    x1_ref[...] = x1
    hx = _rms_rows(x1, nx_ref[...]).astype(BF16)
    qx = _dot(hx, wq_ref[...])
    qx_ref[...] = _head_rms(qx, qn_ref[...], x_dim).astype(qx_ref.dtype)


def _outproj(x, o, y, w_out, norm_x, wq_x, qx_norm, tm):
    t, d = x.shape
    a_width = o.shape[1]
    x_width = wq_x.shape[1]
    row = lambda w: pl.BlockSpec((tm, w), lambda m: (m, 0))
    const = lambda a: pl.BlockSpec(a.shape, lambda m: (0, 0))
    return pl.pallas_call(
        functools.partial(_outproj_kernel, a_width=a_width, x_dim=qx_norm.shape[1]),
        grid=(t // tm,),
        in_specs=[row(d), row(a_width), row(y.shape[1]), const(w_out), const(norm_x), const(wq_x),
                  const(qx_norm)],
        out_specs=[row(d), row(x_width)],
        out_shape=[jax.ShapeDtypeStruct((t, d), F32), jax.ShapeDtypeStruct((t, x_width), BF16)],
        compiler_params=_params(("parallel",)),
        name="out_proj",
    )(x, o, y, w_out, norm_x, wq_x, qx_norm)


def _memkv_kernel(mem_ref, g_ref, w_ref, kn_ref, mk_ref, mv_ref, *, x_dim):
    m = _rms_rows(mem_ref[...], g_ref[...]).astype(BF16)
    kv = _dot(m, w_ref[...])
    xw = mk_ref.shape[1]
    mk_ref[...] = _head_rms(kv[:, 0:xw], kn_ref[...], x_dim)
    mv_ref[...] = kv[:, xw:]


def _memkv(mem, norm_mem, w_kv, kx_norm, tm):
    t, d = mem.shape
    xw = w_kv.shape[1] // 2
    const = lambda a: pl.BlockSpec(a.shape, lambda m: (0, 0))
    return pl.pallas_call(
        functools.partial(_memkv_kernel, x_dim=kx_norm.shape[1]),
        grid=(t // tm,),
        in_specs=[pl.BlockSpec((tm, d), lambda m: (m, 0)), const(norm_mem), const(w_kv), const(kx_norm)],
        out_specs=[pl.BlockSpec((tm, xw), lambda m: (m, 0))] * 2,
        out_shape=[jax.ShapeDtypeStruct((t, xw), F32)] * 2,
        compiler_params=_params(("parallel",)),
        name="memory_kv",
    )(mem, norm_mem, w_kv, kx_norm)


def _xattn_heads(q, mk, mv, x_dim):
    scale = x_dim ** -0.5
    outs = []
    for h in range(q.shape[1] // x_dim):
        sl = slice(h * x_dim, (h + 1) * x_dim)
        s = _dot_nt(q[:, sl], mk[:, sl].astype(BF16)) * scale
        s = s - jnp.max(s, axis=-1, keepdims=True)
        e = jnp.exp(s)
        p = e / jnp.sum(e, axis=-1, keepdims=True)
        outs.append(_dot(p.astype(BF16), mv[:, sl].astype(BF16)))
    return jnp.concatenate(outs, axis=1)


def _xattn_prompt_kernel(q_ref, mk_ref, mv_ref, o_ref, *, x_dim):
    o_ref[...] = _xattn_heads(q_ref[...], mk_ref[...], mv_ref[...], x_dim).astype(o_ref.dtype)


def _xattn_prompt(qx, mk, mv, batch, seq, n_mem, x_dim, tq):
    t, xw = qx.shape
    nq = seq // tq
    return pl.pallas_call(
        functools.partial(_xattn_prompt_kernel, x_dim=x_dim),
        grid=(batch, nq),
        in_specs=[pl.BlockSpec((tq, xw), lambda b, i: (b * nq + i, 0)),
                  pl.BlockSpec((n_mem, xw), lambda b, i: (b, 0)),
                  pl.BlockSpec((n_mem, xw), lambda b, i: (b, 0))],
        out_specs=pl.BlockSpec((tq, xw), lambda b, i: (b * nq + i, 0)),
        out_shape=jax.ShapeDtypeStruct((t, xw), BF16),
        compiler_params=_params(("parallel", "parallel")),
        name="xattn_prompt",
    )(qx, mk, mv)


def _xattn_decode_kernel(q_ref, mk_ref, mv_ref, o_ref, *, x_dim):
    for i in range(q_ref.shape[0]):
        q = jnp.broadcast_to(q_ref[i], (2 * SUBLANES, q_ref.shape[2])).astype(BF16)
        o = _xattn_heads(q, mk_ref[i], mv_ref[i], x_dim)
        o_ref[i] = o[0:1, :]


def _xattn_decode(qx, mem_k, mem_v, x_dim, per_step):
    bs, xw = qx.shape
    n_mem = mem_k.shape[1]
    q3 = qx.astype(F32).reshape(bs, 1, xw)
    out = pl.pallas_call(
        functools.partial(_xattn_decode_kernel, x_dim=x_dim),
        grid=(bs // per_step,),
        in_specs=[pl.BlockSpec((per_step, 1, xw), lambda b: (b, 0, 0)),
                  pl.BlockSpec((per_step, n_mem, xw), lambda b: (b, 0, 0)),
                  pl.BlockSpec((per_step, n_mem, xw), lambda b: (b, 0, 0))],
        out_specs=pl.BlockSpec((per_step, 1, xw), lambda b: (b, 0, 0)),
        out_shape=jax.ShapeDtypeStruct((bs, 1, xw), F32),
        compiler_params=_params(("parallel",)),
        name="xattn_decode",
    )(q3, mem_k, mem_v)
    return out.reshape(bs, xw).astype(BF16)


def _xproj_kernel(x1_ref, ox_ref, wo_ref, nf_ref, wq_ref, x2_ref, hf_ref, pq_ref):
    x2 = x1_ref[...] + _dot(ox_ref[...], wo_ref[...])
    x2_ref[...] = x2
    hf = _rms_rows(x2, nf_ref[...]).astype(BF16)
    hf_ref[...] = hf
    pq_ref[...] = _dot(hf, wq_ref[...])


def _xproj(x1, ox, wo_x, norm_ffn, peer_wq, tm):
    t, d = x1.shape
    row = lambda w: pl.BlockSpec((tm, w), lambda m: (m, 0))
    const = lambda a: pl.BlockSpec(a.shape, lambda m: (0, 0))
    return pl.pallas_call(
        _xproj_kernel,
        grid=(t // tm,),
        in_specs=[row(d), row(ox.shape[1]), const(wo_x), const(norm_ffn), const(peer_wq)],
        out_specs=[row(d), row(d), row(peer_wq.shape[1])],
        out_shape=[jax.ShapeDtypeStruct((t, d), F32), jax.ShapeDtypeStruct((t, d), BF16),
                   jax.ShapeDtypeStruct((t, peer_wq.shape[1]), F32)],
        compiler_params=_params(("parallel",)),
        name="xattn_out_proj",
    )(x1, ox, wo_x, norm_ffn, peer_wq)


def _extract_max(s, order):
    m = jnp.max(s, axis=0, keepdims=True)
    big = jnp.float32(1e9)
    first = jnp.min(jnp.where(s == m, order, big), axis=0, keepdims=True)
    return m, first, order == first


def _cand_blocks():
    blocks = [(0, 0), (SUBLANES, 0)]
    blocks += [(0, b) for b in range(1, SUBLANES)]
    return blocks


def _route_kernel(pq_ref, keys_ref, io_ref, jo_ref, go_ref, sc_ref, i_ref, j_ref, g_ref, *, heads, nkeys):
    tt = pq_ref.shape[0]
    k = P_TOPK
    neg = jnp.float32(-jnp.inf)
    order1 = lax.broadcasted_iota(I32, (nkeys, tt), 0).astype(F32)
    pq = pq_ref[...].astype(BF16)
    for h in range(heads):
        sc_ref[h] = _dot_nt(keys_ref[h], pq[:, h * LANES:(h + 1) * LANES])

    def head_body(h, carry):
        sv, si = [], []
        for c in range(2):
            s = sc_ref[h, c * nkeys:(c + 1) * nkeys, :]
            vals, idxs = [], []
            for _ in range(k):
                m, first, hit = _extract_max(s, order1)
                vals.append(m)
                idxs.append(first)
                s = jnp.where(hit, neg, s)
            sv.append(jnp.concatenate(vals, axis=0))
            si.append(jnp.concatenate(idxs, axis=0))
        cand, ci, cj, flat = [], [], [], []
        row8 = lax.broadcasted_iota(I32, (SUBLANES, tt), 0).astype(F32)
        for a0, b in _cand_blocks():
            cand.append(sv[0][a0:a0 + SUBLANES] + sv[1][b:b + 1])
            ci.append(si[0][a0:a0 + SUBLANES])
            cj.append(jnp.broadcast_to(si[1][b:b + 1], (SUBLANES, tt)))
            flat.append((row8 + a0) * k + b)
        cand.append(sv[0][0:1] + sv[1][SUBLANES:k])
        ci.append(jnp.broadcast_to(si[0][0:1], (SUBLANES, tt)))
        cj.append(si[1][SUBLANES:k])
        flat.append(row8 + SUBLANES)
        cand, ci, cj, flat = (jnp.concatenate(v, axis=0) for v in (cand, ci, cj, flat))
        fs, fi, fj = [], [], []
        for _ in range(k):
            m, _, hit = _extract_max(cand, flat)
            fs.append(m)
            fi.append(jnp.sum(jnp.where(hit, ci, 0.0), axis=0, keepdims=True))
            fj.append(jnp.sum(jnp.where(hit, cj, 0.0), axis=0, keepdims=True))
            cand = jnp.where(hit, neg, cand)
        fs, fi, fj = (jnp.concatenate(v, axis=0) for v in (fs, fi, fj))
        e = jnp.exp(fs - fs[0:1])
        gate = e / jnp.sum(e, axis=0, keepdims=True)
        r0 = pl.multiple_of(h * k, k)
        i_ref[pl.ds(r0, k), :] = fi
        j_ref[pl.ds(r0, k), :] = fj
        g_ref[pl.ds(r0, k), :] = gate
        return carry

    lax.fori_loop(0, heads, head_body, 0)
    io_ref[...] = i_ref[...].T.astype(I32)
    jo_ref[...] = j_ref[...].T.astype(I32)
    go_ref[...] = g_ref[...].T


def _route(pq, keys_pad, tt):
    t = pq.shape[0]
    heads, two_nkeys, _ = keys_pad.shape
    nkeys = two_nkeys // 2
    hk = heads * P_TOPK
    outs = pl.pallas_call(
        functools.partial(_route_kernel, heads=heads, nkeys=nkeys),
        grid=(t // tt,),
        in_specs=[pl.BlockSpec((tt, pq.shape[1]), lambda m: (m, 0)),
                  pl.BlockSpec(keys_pad.shape, lambda m: (0, 0, 0))],
        out_specs=[pl.BlockSpec((tt, hk), lambda m: (m, 0))] * 3,
        out_shape=[jax.ShapeDtypeStruct((t, hk), I32), jax.ShapeDtypeStruct((t, hk), I32),
                   jax.ShapeDtypeStruct((t, hk), F32)],
        scratch_shapes=[pltpu.VMEM((heads, two_nkeys, tt), F32)] + [pltpu.VMEM((hk, tt), F32)] * 3,
        compiler_params=_params(("parallel",)),
        name="peer_route",
    )(pq, keys_pad)
    return outs


def _peer_act_kernel(hf_ref, u_ref, i_ref, j_ref, act_ref, *, nkeys):
    e = pl.program_id(1)

    @pl.when(e == 0)
    def _():
        act_ref[...] = jnp.zeros(act_ref.shape, F32)

    s = _dot_nt(hf_ref[...], u_ref[...])
    ii = i_ref[...]
    jj = j_ref[...]
    act = act_ref[...]
    per = u_ref.shape[0] // nkeys
    for c in range(per):
        picked = jnp.take_along_axis(s[:, c * nkeys:(c + 1) * nkeys], jj, axis=1)
        act = jnp.where(ii == e * per + c, picked, act)
    act_ref[...] = act


def _peer_act(hf, u_bf16, ii, jj, nkeys, tt, eb):
    t, d = hf.shape
    hk = ii.shape[1]
    return pl.pallas_call(
        functools.partial(_peer_act_kernel, nkeys=nkeys),
        grid=(t // tt, u_bf16.shape[0] // eb),
        in_specs=[pl.BlockSpec((tt, d), lambda m, e: (m, 0)),
                  pl.BlockSpec((eb, d), lambda m, e: (e, 0)),
                  pl.BlockSpec((tt, hk), lambda m, e: (m, 0)),
                  pl.BlockSpec((tt, hk), lambda m, e: (m, 0))],
        out_specs=pl.BlockSpec((tt, hk), lambda m, e: (m, 0)),
        out_shape=jax.ShapeDtypeStruct((t, hk), F32),
        compiler_params=_params(("parallel", "arbitrary")),
        name="peer_act",
    )(hf, u_bf16, ii, jj)


def _peer_out_kernel(act_ref, g_ref, i_ref, j_ref, x_ref, v_ref, y_ref, wd_ref, acc_ref, w_ref, *, nkeys):
    e = pl.program_id(1)
    tt = act_ref.shape[0]
    hk = act_ref.shape[1]
    per = v_ref.shape[0] // nkeys

    @pl.when(e == 0)
    def _():
        a = act_ref[...]
        gelu = 0.5 * a * (1.0 + lax.erf(a * (2.0 ** -0.5)))
        w_ref[...] = g_ref[...] * gelu
        acc_ref[...] = x_ref[...]
        sub = lax.broadcasted_iota(I32, (nkeys, hk), 0)

        def group(gidx, carry):
            t0 = pl.multiple_of(gidx * SUBLANES, SUBLANES)
            w8 = w_ref[pl.ds(t0, SUBLANES), :]
            i8 = i_ref[pl.ds(t0, SUBLANES), :]
            j8 = j_ref[pl.ds(t0, SUBLANES), :]
            for r in range(SUBLANES):
                wi = jnp.where(sub == i8[r:r + 1], w8[r:r + 1], 0.0).astype(BF16)
                oj = jnp.where(sub == j8[r:r + 1], 1.0, 0.0).astype(BF16)
                wd_ref[pl.ds(t0 + r, nkeys, stride=tt), :] = _dot_nt(wi, oj)
            return carry

        lax.fori_loop(0, tt // SUBLANES, group, 0)

    acc = acc_ref[...]
    for c in range(per):
        r0 = pl.multiple_of((e * per + c) * tt, tt)
        lhs = wd_ref[pl.ds(r0, tt), :].astype(BF16)
        acc = acc + _dot(lhs, v_ref[c * nkeys:(c + 1) * nkeys, :])
    acc_ref[...] = acc

    @pl.when(e == pl.num_programs(1) - 1)
    def _():
        y_ref[...] = acc_ref[...]


def _peer_out(act, gate, ii, jj, x2, v_bf16, nkeys, tt, eb):
    t, d = x2.shape
    hk = act.shape[1]
    tok = lambda w: pl.BlockSpec((tt, w), lambda m, e: (m, 0))
    return pl.pallas_call(
        functools.partial(_peer_out_kernel, nkeys=nkeys),
        grid=(t // tt, v_bf16.shape[0] // eb),
        in_specs=[tok(hk), tok(hk), tok(hk), tok(hk), tok(d),
                  pl.BlockSpec((eb, d), lambda m, e: (e, 0))],
        out_specs=tok(d),
        out_shape=jax.ShapeDtypeStruct((t, d), F32),
        scratch_shapes=[pltpu.VMEM((nkeys * tt, nkeys), F32), pltpu.VMEM((tt, d), F32),
                        pltpu.VMEM((tt, hk), F32)],
        compiler_params=_params(("parallel", "arbitrary")),
        name="peer_out",
    )(act, gate, ii, jj, x2, v_bf16)


def _tile(n, pref):
    return pref if n % pref == 0 else n


def _layer_consts(lw, a_dim, heads_s, head_dim, inner, conv_dim):
    d = lw['w_in'].shape[0]
    a_width = lw['w_out'].shape[0] - inner
    tn = a_width
    cols = lw['w_in'].shape[1]
    assert cols - 4 * tn - conv_dim == heads_s <= LANES
    n_pad = 4 * tn + -(-(conv_dim + LANES) // tn) * tn
    w_pad = jnp.zeros((d, n_pad), BF16).at[:, :cols].set(lw['w_in'].astype(BF16))
    reps = a_width // a_dim
    colgain = jnp.concatenate([jnp.tile(lw['q_norm'], reps) * (a_dim ** -0.5),
                               jnp.tile(lw['k_norm'], reps)])[None, :]
    gid = jnp.arange(LANES) // a_dim
    bd = (gid[:, None] == gid[None, :]).astype(BF16)
    pad_h = lambda v: jnp.zeros((1, LANES), F32).at[0, :heads_s].set(v)
    tri = (jnp.arange(S_CHUNK)[None, :] <= jnp.arange(S_CHUNK)[:, None]).astype(BF16)
    expand = (jnp.arange(LANES)[:, None] == (jnp.arange(inner) // head_dim)[None, :]).astype(BF16)
    keys = lw['peer_keys']
    _, p_heads, nkeys, half = keys.shape
    keys_pad = jnp.zeros((p_heads, 2 * nkeys, 2 * half), BF16)
    keys_pad = keys_pad.at[:, :nkeys, :half].set(keys[0].astype(BF16))
    keys_pad = keys_pad.at[:, nkeys:, half:].set(keys[1].astype(BF16))
    return dict(
        w_pad=w_pad, colgain=colgain, bd=bd, tri=tri, expand=expand, keys_pad=keys_pad,
        norm_mix=lw['norm_mix'][None, :], subln=lw['subln'][None, :],
        lams=tuple(lw[n][None, :] for n in ('lambda_q1', 'lambda_k1', 'lambda_q2', 'lambda_k2')),
        conv_w=lw['conv_w'], conv_b=lw['conv_b'][None, :],
        dtb=pad_h(lw['dt_bias']), alog=pad_h(lw['a_log']),
        dskip=jnp.repeat(lw['d_skip'], head_dim)[None, :], ssm_norm=lw['ssm_norm'][None, :],
        w_out=lw['w_out'].astype(BF16), norm_x=lw['norm_x'][None, :], wq_x=lw['wq_x'].astype(BF16),
        qx_norm=lw['qx_norm'][None, :], kx_norm=lw['kx_norm'][None, :], norm_mem=lw['norm_mem'][None, :],
        w_kv=jnp.concatenate([lw['wk_x'], lw['wv_x']], axis=1).astype(BF16),
        wo_x=lw['wo_x'].astype(BF16), norm_ffn=lw['norm_ffn'][None, :],
        peer_wq=lw['peer_wq'].astype(BF16), peer_u=lw['peer_u'].astype(BF16),
        peer_v=lw['peer_v'].astype(BF16), nkeys=nkeys,
    )


def _peer(c, x2, hf, pq, tt_act, tt_out):
    t = x2.shape[0]
    nkeys = c['nkeys']
    ii, jj, gate = _route(pq, c['keys_pad'], LANES)
    eb = 2 * nkeys
    act = _peer_act(hf, c['peer_u'], ii, jj, nkeys, _tile(t, tt_act), eb)
    return _peer_out(act, gate, ii, jj, x2, c['peer_v'], nkeys, _tile(t, tt_out), eb)


def kernel(x_prompt, x_sample, cache_k, cache_v, cache_mem_k, cache_mem_v, state_conv, state_ssm, page_table, mem_prompt, norm_mix, w_in, q_norm, k_norm, lambda_q1, lambda_k1, lambda_q2, lambda_k2, subln, conv_w, conv_b, dt_bias, a_log, d_skip, ssm_norm, w_out, norm_x, norm_mem, wq_x, wk_x, wv_x, qx_norm, kx_norm, wo_x, norm_ffn, peer_wq, peer_keys, peer_u, peer_v):
    weights = dict(norm_mix=norm_mix, w_in=w_in, q_norm=q_norm, k_norm=k_norm, lambda_q1=lambda_q1,
                   lambda_k1=lambda_k1, lambda_q2=lambda_q2, lambda_k2=lambda_k2, subln=subln,
                   conv_w=conv_w, conv_b=conv_b, dt_bias=dt_bias, a_log=a_log, d_skip=d_skip,
                   ssm_norm=ssm_norm, w_out=w_out, norm_x=norm_x, norm_mem=norm_mem, wq_x=wq_x,
                   wk_x=wk_x, wv_x=wv_x, qx_norm=qx_norm, kx_norm=kx_norm, wo_x=wo_x,
                   norm_ffn=norm_ffn, peer_wq=peer_wq, peer_keys=peer_keys, peer_u=peer_u, peer_v=peer_v)
    depth = w_in.shape[0]
    bp, seq, d = x_prompt.shape
    bs, dec_seq, _ = x_sample.shape
    assert dec_seq == 1
    a_dim = q_norm.shape[-1]
    a_heads = cache_k.shape[3]
    a_width = a_heads * 2 * a_dim
    s_heads, head_dim, state = state_ssm.shape[2:]
    inner = s_heads * head_dim
    conv_dim = state_conv.shape[-1]
    n_mem, x_heads, x_dim = cache_mem_k.shape[2:]
    x_width = x_heads * x_dim
    assert a_width == inner and 2 * a_dim == LANES and 2 * head_dim == LANES and state == LANES

    yp = x_prompt.reshape(bp * seq, d)
    ys = x_sample.reshape(bs, d)
    outs = {n: [] for n in ('kp', 'vp', 'cp', 'sp', 'mkp', 'mvp', 'ks', 'vs', 'cs', 'ss')}
    for l in range(depth):
        lw = {n: w[l] for n, w in weights.items()}
        c = _layer_consts(lw, a_dim, s_heads, head_dim, inner, conv_dim)
        lam0 = 0.8 - 0.6 * math.exp(-0.3 * l)

        mk, mv = _memkv(mem_prompt.reshape(bp * n_mem, d), c['norm_mem'], c['w_kv'], c['kx_norm'],
                        _tile(bp * n_mem, 256))
        q, k, v, z, xd = _inproj(yp, c['norm_mix'], c['w_pad'], c['colgain'], c['bd'], a_dim,
                                 _tile(bp * seq, 512))
        o = _attn_prompt(q, k, v, c['subln'], c['lams'], bp, seq, a_dim, lam0, _tile(seq, 256))
        ym, st = _ssd_prompt(xd, z, c['conv_w'], c['conv_b'], c['dtb'], c['alog'], c['dskip'],
                             c['ssm_norm'], c['tri'], bp, seq, inner, state, conv_dim, head_dim)
        x1, qx = _outproj(yp, o, ym, c['w_out'], c['norm_x'], c['wq_x'], c['qx_norm'], _tile(bp * seq, 256))
        ox = _xattn_prompt(qx, mk, mv, bp, seq, n_mem, x_dim, _tile(seq, 256))
        x2, hf, pq = _xproj(x1, ox, c['wo_x'], c['norm_ffn'], c['peer_wq'], _tile(bp * seq, 256))
        yp = _peer(c, x2, hf, pq, 512, 256)
        outs['kp'].append(k.reshape(bp, seq, a_heads, 2 * a_dim))
        outs['vp'].append(v.reshape(bp, seq, a_heads, 2 * a_dim))
        outs['cp'].append(xd.reshape(bp, seq, -1)[:, seq - (conv_w.shape[1] - 1):, :conv_dim])
        outs['sp'].append(st.reshape(bp, s_heads, head_dim, state))
        outs['mkp'].append(mk.reshape(bp, n_mem, x_heads, x_dim))
        outs['mvp'].append(mv.reshape(bp, n_mem, x_heads, x_dim))

        q, k, v, z, xd = _inproj(ys, c['norm_mix'], c['w_pad'], c['colgain'], c['bd'], a_dim, bs)
        page = cache_k.shape[2]
        ck = cache_k[l].reshape(-1, page, a_width)
        cv = cache_v[l].reshape(-1, page, a_width)
        o = _attn_decode(q, k, v, ck, cv, page_table, c['subln'], c['lams'], a_dim, lam0).astype(BF16)
        conv_prev = state_conv[l]
        xbc, xdt_t, da = _ssd_step_prep(xd, jnp.swapaxes(conv_prev, 0, 1), c['conv_w'], c['conv_b'],
                                        c['dtb'], c['alog'], c['expand'], inner, conv_dim)
        hn, yt = _ssd_step(da[:, :s_heads].reshape(-1), state_ssm[l].reshape(bs, inner, state), xdt_t,
                           xbc[:, inner:], state, head_dim)
        ym = _ssd_step_out(yt, xbc, z, c['dskip'], c['ssm_norm'], inner)
        x1, qx = _outproj(ys, o, ym, c['w_out'], c['norm_x'], c['wq_x'], c['qx_norm'], bs)
        ox = _xattn_decode(qx, cache_mem_k[l].reshape(bs, n_mem, x_width),
                           cache_mem_v[l].reshape(bs, n_mem, x_width), x_dim, 4)
        x2, hf, pq = _xproj(x1, ox, c['wo_x'], c['norm_ffn'], c['peer_wq'], bs)
        ys = _peer(c, x2, hf, pq, bs, bs)
        outs['ks'].append(k.reshape(bs, 1, a_heads, 2 * a_dim))
        outs['vs'].append(v.reshape(bs, 1, a_heads, 2 * a_dim))
        outs['cs'].append(jnp.concatenate([conv_prev[:, 1:], xd[:, None, :conv_dim]], axis=1))
        outs['ss'].append(hn.reshape(bs, s_heads, head_dim, state))

    st = lambda n: jnp.stack(outs[n])
    return (yp.reshape(bp, seq, d), ys.reshape(bs, 1, d), st('kp'), st('vp'), st('cp'), st('sp'),
            st('mkp'), st('mvp'), st('ks'), st('vs'), st('cs'), st('ss'))
```

```python
import functools
import math

import jax
import jax.numpy as jnp
from jax import lax
from jax.experimental import pallas as pl
from jax.experimental.pallas import tpu as pltpu

F32 = jnp.float32
BF16 = jnp.bfloat16
I32 = jnp.int32

EPS = 1e-6
LANES = 128
SUBLANES = 8
S_GROUPS = 2
S_CHUNK = 128
P_TOPK = 16
PAGES_PER_STEP = 8
PEER_ACT_BLOCKS = 8
PEER_OUT_BLOCKS = 4
VMEM_LIMIT = 56 << 20

NT_DIMS = (((1,), (1,)), ((), ()))


def _params(sem):
    return pltpu.CompilerParams(dimension_semantics=sem, vmem_limit_bytes=VMEM_LIMIT)


def _dot(a, b):
    return jnp.dot(a, b, preferred_element_type=F32)


def _dot_nt(a, b):
    return lax.dot_general(a, b, NT_DIMS, preferred_element_type=F32)


def _split3(x):
    hi = x.astype(BF16)
    r1 = x - hi.astype(F32)
    mid = r1.astype(BF16)
    lo = (r1 - mid.astype(F32)).astype(BF16)
    return hi, mid, lo


def _dot3(x, m_bf16):
    hi, mid, lo = _split3(x)
    return _dot(hi, m_bf16) + _dot(mid, m_bf16) + _dot(lo, m_bf16)


def _rms_rows(x, gain):
    ms = jnp.mean(x * x, axis=-1, keepdims=True)
    return x * lax.rsqrt(ms + EPS) * gain


def _head_rms(x, gain, width):
    outs = []
    for h in range(x.shape[1] // width):
        c = x[:, h * width:(h + 1) * width]
        outs.append(_rms_rows(c, gain))
    return outs[0] if len(outs) == 1 else jnp.concatenate(outs, axis=1)


def _silu(x):
    return x * (1.0 / (1.0 + jnp.exp(-x)))


def _softplus(x):
    return jnp.maximum(x, 0.0) + jnp.log1p(jnp.exp(-jnp.abs(x)))


def _inproj_kernel(x_ref, g_ref, w_ref, cg_ref, bd_ref, q_ref, k_ref, v_ref, z_ref, xd_ref,
                   xn_ref, *, group):
    n = pl.program_id(1)

    @pl.when(n == 0)
    def _():
        xn_ref[...] = _rms_rows(x_ref[...], g_ref[...]).astype(BF16)

    acc = _dot(xn_ref[...], w_ref[...])
    tn = acc.shape[1]

    def group_norm(dst_ref):
        for c in range(tn // LANES):
            a = acc[:, c * LANES:(c + 1) * LANES]
            sq = a * a
            hi = sq.astype(BF16)
            lo = (sq - hi.astype(F32)).astype(BF16)
            ss = _dot(hi, bd_ref[...]) + _dot(lo, bd_ref[...])
            y = a * lax.rsqrt(ss * (1.0 / group) + EPS) * cg_ref[:, c * LANES:(c + 1) * LANES]
            dst_ref[:, c * LANES:(c + 1) * LANES] = y.astype(dst_ref.dtype)

    @pl.when(n == 0)
    def _():
        group_norm(q_ref)

    @pl.when(n == 1)
    def _():
        group_norm(k_ref)

    @pl.when(n == 2)
    def _():
        v_ref[...] = acc

    @pl.when(n == 3)
    def _():
        z_ref[...] = acc

    @pl.when(n >= 4)
    def _():
        xd_ref[...] = acc


def _inproj(x, gain, w_pad, colgain, bd, a_dim, tm):
    t, d = x.shape
    n_pad = w_pad.shape[1]
    tn = colgain.shape[1] // 2
    nt = n_pad // tn
    nx = nt - 4
    grid = (t // tm, nt)
    row = lambda m, n: (m, 0)
    return pl.pallas_call(
        functools.partial(_inproj_kernel, group=a_dim),
        grid=grid,
        in_specs=[
            pl.BlockSpec((tm, d), row),
            pl.BlockSpec((1, d), lambda m, n: (0, 0)),
            pl.BlockSpec((d, tn), lambda m, n: (0, n)),
            pl.BlockSpec((1, tn), lambda m, n: (0, jnp.minimum(n, 1))),
            pl.BlockSpec((LANES, LANES), lambda m, n: (0, 0)),
        ],
        out_specs=[
            pl.BlockSpec((tm, tn), row),
            pl.BlockSpec((tm, tn), row),
            pl.BlockSpec((tm, tn), row),
            pl.BlockSpec((tm, tn), row),
            pl.BlockSpec((tm, tn), lambda m, n: (m, jnp.clip(n - 4, 0, nx - 1))),
        ],
        out_shape=[
            jax.ShapeDtypeStruct((t, tn), BF16),
            jax.ShapeDtypeStruct((t, tn), F32),
            jax.ShapeDtypeStruct((t, tn), F32),
            jax.ShapeDtypeStruct((t, tn), F32),
            jax.ShapeDtypeStruct((t, nx * tn), F32),
        ],
        scratch_shapes=[pltpu.VMEM((tm, d), BF16)],
        compiler_params=_params(("parallel", "arbitrary")),
        name="in_proj",
    )(x, gain, w_pad, colgain, bd)


def _lambda(lq1, lk1, lq2, lk2, lam0):
    return (jnp.exp(jnp.sum(lq1[...] * lk1[...], axis=-1, keepdims=True))
            - jnp.exp(jnp.sum(lq2[...] * lk2[...], axis=-1, keepdims=True)) + lam0)


def _attn_prompt_kernel(q_ref, k_ref, v_ref, sub_ref, lq1, lk1, lq2, lk2, o_ref,
                        kb_ref, vt_ref, m_ref, l_ref, acc_ref, *, a_dim, lam0):
    qi = pl.program_id(2)
    tq = q_ref.shape[0]
    n_kv = vt_ref.shape[0]

    @pl.when(qi == 0)
    def _():
        kb_ref[...] = k_ref[...].astype(BF16)
        for j in range(n_kv):
            vt_ref[j] = v_ref[j * tq:(j + 1) * tq, :].T.astype(BF16)

    q = q_ref[...]
    lane = lax.broadcasted_iota(I32, q.shape, 1)
    zero = jnp.zeros_like(q)
    q2 = jnp.concatenate([jnp.where(lane < a_dim, q, zero), jnp.where(lane >= a_dim, q, zero)], axis=0)

    m_ref[...] = jnp.full(m_ref.shape, -jnp.inf, F32)
    l_ref[...] = jnp.zeros(l_ref.shape, F32)
    acc_ref[...] = jnp.zeros(acc_ref.shape, F32)

    def step(j, masked):
        off = pl.multiple_of(j * tq, tq)
        s = _dot_nt(kb_ref[pl.ds(off, tq), :], q2)
        if masked:
            r = lax.broadcasted_iota(I32, s.shape, 0)
            c = lax.broadcasted_iota(I32, s.shape, 1)
            c = jnp.where(c >= tq, c - tq, c)
            s = jnp.where(r <= c, s, -jnp.inf)
        m_old = m_ref[...]
        m_new = jnp.maximum(m_old, jnp.max(s, axis=0, keepdims=True))
        alpha = jnp.exp(m_old - m_new)
        p = jnp.exp(s - m_new)
        l_ref[...] = alpha * l_ref[...] + jnp.sum(p, axis=0, keepdims=True)
        acc_ref[...] = alpha * acc_ref[...] + _dot(vt_ref[j], p.astype(BF16))
        m_ref[...] = m_new

    def body(j, carry):
        step(j, False)
        return carry

    lax.fori_loop(0, qi, body, 0)
    step(qi, True)

    lam = _lambda(lq1, lk1, lq2, lk2, lam0)
    o1 = acc_ref[:, 0:tq] / l_ref[:, 0:tq]
    o2 = acc_ref[:, tq:2 * tq] / l_ref[:, tq:2 * tq]
    o = o1 - lam * o2
    ms = jnp.mean(o * o, axis=0, keepdims=True)
    o = o * lax.rsqrt(ms + EPS) * sub_ref[...] * (1.0 - lam0)
    o_ref[...] = o.T.astype(o_ref.dtype)


def _attn_prompt(q, k, v, subln_col, lams, batch, seq, a_dim, lam0, tq):
    t, width = q.shape
    hd = 2 * a_dim
    heads = width // hd
    nq = seq // tq
    small = pl.BlockSpec((1, a_dim), lambda b, h, i: (0, 0))
    return pl.pallas_call(
        functools.partial(_attn_prompt_kernel, a_dim=a_dim, lam0=lam0),
        grid=(batch, heads, nq),
        in_specs=[
            pl.BlockSpec((tq, hd), lambda b, h, i: (b * nq + i, h)),
            pl.BlockSpec((seq, hd), lambda b, h, i: (b, h)),
            pl.BlockSpec((seq, hd), lambda b, h, i: (b, h)),
            pl.BlockSpec((hd, 1), lambda b, h, i: (0, 0)),
            small, small, small, small,
        ],
        out_specs=pl.BlockSpec((tq, hd), lambda b, h, i: (b * nq + i, h)),
        out_shape=jax.ShapeDtypeStruct((t, width), BF16),
        scratch_shapes=[pltpu.VMEM((seq, hd), BF16), pltpu.VMEM((nq, hd, tq), BF16),
                        pltpu.VMEM((1, 2 * tq), F32), pltpu.VMEM((1, 2 * tq), F32),
                        pltpu.VMEM((hd, 2 * tq), F32)],
        compiler_params=_params(("parallel", "parallel", "arbitrary")),
        name="diff_attn_prompt",
    )(q, k, v, subln_col, *lams)


def _attn_decode_kernel(pt_ref, q_ref, kn_ref, vn_ref, sub_ref, lq1, lk1, lq2, lk2, *rest,
                        a_dim, lam0, heads):
    del pt_ref
    kp = rest[:PAGES_PER_STEP]
    vp = rest[PAGES_PER_STEP:2 * PAGES_PER_STEP]
    o_ref, qr_ref, m_ref, l_ref, acc_ref = rest[2 * PAGES_PER_STEP:]
    p_id = pl.program_id(1)
    rows = 2 * heads
    hd = 2 * a_dim
    page = kp[0].shape[2]
    cols = page * heads

    @pl.when(p_id == 0)
    def _():
        q8 = q_ref[0]
        lane = lax.broadcasted_iota(I32, q8.shape, 1)
        qr_ref[...] = jnp.concatenate([jnp.where(lane < a_dim, q8, 0.0),
                                       jnp.where(lane >= a_dim, q8, 0.0)], axis=0).astype(BF16)
        m_ref[...] = jnp.full(m_ref.shape, -jnp.inf, F32)
        l_ref[...] = jnp.zeros(l_ref.shape, F32)
        acc_ref[...] = jnp.zeros(acc_ref.shape, F32)

    qr = qr_ref[...]
    r = lax.broadcasted_iota(I32, (rows, cols), 0)
    c = lax.broadcasted_iota(I32, (rows, cols), 1)
    same_head = (r % heads) == (c % heads)

    s = [jnp.where(same_head, _dot_nt(qr, kp[i][0, 0].reshape(cols, hd).astype(BF16)), -jnp.inf)
         for i in range(PAGES_PER_STEP)]
    m_old = m_ref[...]
    m_new = m_old
    for si in s:
        m_new = jnp.maximum(m_new, jnp.max(si, axis=-1, keepdims=True))
    alpha = jnp.exp(m_old - m_new)
    l_new = alpha * l_ref[...]
    acc = alpha * acc_ref[...]
    for i, si in enumerate(s):
        p = jnp.exp(si - m_new)
        l_new = l_new + jnp.sum(p, axis=-1, keepdims=True)
        acc = acc + _dot(p.astype(BF16), vp[i][0, 0].reshape(cols, hd).astype(BF16))
    m_ref[...] = m_new
    l_ref[...] = l_new
    acc_ref[...] = acc

    @pl.when(p_id == pl.num_programs(1) - 1)
    def _():
        kn = kn_ref[0].astype(BF16).astype(F32)
        vn = vn_ref[0].astype(BF16).astype(F32)
        kn2 = jnp.concatenate([kn, kn], axis=0)
        vn2 = jnp.concatenate([vn, vn], axis=0)
        s_new = jnp.sum(qr.astype(F32) * kn2, axis=-1, keepdims=True)
        m_fin = jnp.maximum(m_new, s_new)
        a_fin = jnp.exp(m_new - m_fin)
        p_new = jnp.exp(s_new - m_fin)
        l_fin = a_fin * l_new + p_new
        acc_fin = a_fin * acc + p_new.astype(BF16).astype(F32) * vn2

        lam = _lambda(lq1, lk1, lq2, lk2, lam0)
        o1 = acc_fin[0:heads, :] / l_fin[0:heads, :]
        o2 = acc_fin[heads:rows, :] / l_fin[heads:rows, :]
        o_ref[0] = _rms_rows(o1 - lam * o2, sub_ref[...]) * (1.0 - lam0)


def _attn_decode(q, k_new, v_new, cache_k, cache_v, layer, page_table, subln, lams, a_dim, lam0):
    bs, width = q.shape
    page, heads, hd = cache_k.shape[2:]
    n_pages = page_table.shape[1]
    steps = n_pages // PAGES_PER_STEP
    pt = page_table.reshape(-1)
    q3, k3, v3 = (a.astype(F32).reshape(bs, heads, hd) for a in (q, k_new, v_new))
    row = pl.BlockSpec((1, heads, hd), lambda b, p, pt: (b, 0, 0))
    small = pl.BlockSpec((1, a_dim), lambda b, p, pt: (0, 0))

    def page_spec(i):
        return pl.BlockSpec((1, 1, page, heads, hd),
                            lambda b, p, pt: (layer, pt[b * n_pages + p * PAGES_PER_STEP + i], 0, 0, 0))

    grid_spec = pltpu.PrefetchScalarGridSpec(
        num_scalar_prefetch=1,
        grid=(bs, steps),
        in_specs=[row, row, row, pl.BlockSpec((1, hd), lambda b, p, pt: (0, 0)),
                  small, small, small, small]
        + [page_spec(i) for i in range(PAGES_PER_STEP)] * 2,
        out_specs=row,
        scratch_shapes=[pltpu.VMEM((2 * heads, hd), BF16), pltpu.VMEM((2 * heads, 1), F32),
                        pltpu.VMEM((2 * heads, 1), F32), pltpu.VMEM((2 * heads, hd), F32)],
    )
    out = pl.pallas_call(
        functools.partial(_attn_decode_kernel, a_dim=a_dim, lam0=lam0, heads=heads),
        grid_spec=grid_spec,
        out_shape=jax.ShapeDtypeStruct((bs, heads, hd), F32),
        compiler_params=_params(("parallel", "arbitrary")),
        name="diff_attn_decode",
    )(pt, q3, k3, v3, subln, *lams, *([cache_k] * PAGES_PER_STEP), *([cache_v] * PAGES_PER_STEP))
    return out.reshape(bs, width)


def _conv_taps(cur, tail, w_ref, b_ref):
    n_tap = w_ref.shape[0]
    out = b_ref[...] + w_ref[n_tap - 1:n_tap, :] * cur
    row8 = lax.broadcasted_iota(I32, (SUBLANES, cur.shape[1]), 0)
    for s in range(1, n_tap):
        rolled = pltpu.roll(cur, s, 0)
        head = jnp.where(row8 < s, pltpu.roll(tail, s, 0), rolled[0:SUBLANES])
        shifted = jnp.concatenate([head, rolled[SUBLANES:]], axis=0)
        out = out + w_ref[n_tap - 1 - s:n_tap - s, :] * shifted
    return out


def _ssd_prompt_kernel(xd_ref, z_ref, cw_ref, cb_ref, dtb_ref, alog_ref, dsk_ref, ng_ref, tri_ref,
                       y_ref, st_ref, tail_ref, h_ref, yg_ref, *, inner, state, conv_dim, head_dim):
    c_id = pl.program_id(1)
    ch = xd_ref.shape[0]
    pairs = h_ref.shape[0]
    rep = (inner // head_dim) // S_GROUPS

    @pl.when(c_id == 0)
    def _():
        tail_ref[...] = jnp.zeros(tail_ref.shape, F32)
        h_ref[...] = jnp.zeros(h_ref.shape, F32)

    pre = xd_ref[:, 0:conv_dim]
    xbc = _silu(_conv_taps(pre, tail_ref[...], cw_ref, cb_ref))
    tail_ref[...] = pre[ch - SUBLANES:ch, :]

    dtv = _softplus(xd_ref[:, conv_dim:conv_dim + LANES] + dtb_ref[...])
    a = dtv * (-jnp.exp(alog_ref[...]))
    a_hi, a_mid, a_lo = _split3(a)
    tri = tri_ref[...]
    a_cum = _dot(tri, a_hi) + _dot(tri, a_mid) + _dot(tri, a_lo)
    a_cum_t = a_cum.T
    row = lax.broadcasted_iota(I32, (ch, ch), 0)
    col = lax.broadcasted_iota(I32, (ch, ch), 1)
    lane = lax.broadcasted_iota(I32, (ch, LANES), 1)
    first = lane < head_dim
    causal = col <= row

    ssq = jnp.zeros((ch, 1), F32)
    bms = [xbc[:, inner + g * state:inner + (g + 1) * state].astype(BF16) for g in range(S_GROUPS)]
    cms = [xbc[:, inner + (S_GROUPS + g) * state:inner + (S_GROUPS + g + 1) * state].astype(BF16)
           for g in range(S_GROUPS)]
    cb = [_dot_nt(cms[g], bms[g]) for g in range(S_GROUPS)]
    for pr in range(pairs):
        g = (2 * pr) // rep
        bm, cm = bms[g], cms[g]
        xs = xbc[:, pr * LANES:(pr + 1) * LANES]
        ha, hb = 2 * pr, 2 * pr + 1
        dt2 = jnp.where(first, dtv[:, ha:ha + 1], dtv[:, hb:hb + 1])
        xdt = xs * dt2
        ac2 = jnp.where(first, a_cum[:, ha:ha + 1], a_cum[:, hb:hb + 1])
        last2 = ac2[ch - 1:ch, :]
        y = jnp.zeros((ch, LANES), F32)
        for hh, keep in ((ha, first), (hb, jnp.logical_not(first))):
            seg = a_cum[:, hh:hh + 1] - a_cum_t[hh:hh + 1, :]
            lm = jnp.exp(jnp.where(causal, seg, -jnp.inf))
            y = y + _dot((cb[g] * lm).astype(BF16), jnp.where(keep, xdt, 0.0).astype(BF16))
        hprev = h_ref[pr]
        y = y + _dot_nt(cm, hprev.astype(BF16)) * jnp.exp(ac2)
        xdec = (xdt * jnp.exp(last2 - ac2)).T
        upd = _dot(xdec.astype(BF16), bm)
        rowp = lax.broadcasted_iota(I32, hprev.shape, 0)
        e_last = jnp.exp(last2)
        decay = jnp.where(rowp < head_dim, e_last[:, 0:1], e_last[:, head_dim:head_dim + 1])
        h_ref[pr] = decay * hprev + upd
        y = y + dsk_ref[:, pr * LANES:(pr + 1) * LANES] * xs
        y = y * _silu(z_ref[:, pr * LANES:(pr + 1) * LANES])
        yg_ref[:, pr * LANES:(pr + 1) * LANES] = y
        ssq = ssq + jnp.sum(y * y, axis=-1, keepdims=True)

    y_ref[...] = (yg_ref[...] * lax.rsqrt(ssq * (1.0 / inner) + EPS) * ng_ref[...]).astype(y_ref.dtype)

    @pl.when(c_id == pl.num_programs(1) - 1)
    def _():
        st_ref[0] = h_ref[...]


def _ssd_prompt(xd, z, conv_w, conv_b, dtb_pad, alog_pad, dskip_exp, norm_g, tri, batch, seq,
                inner, state, conv_dim, head_dim):
    t = xd.shape[0]
    nc = seq // S_CHUNK
    pairs = inner // LANES
    const = lambda shape: pl.BlockSpec(shape, lambda b, c: (0,) * len(shape))
    return pl.pallas_call(
        functools.partial(_ssd_prompt_kernel, inner=inner, state=state, conv_dim=conv_dim,
                          head_dim=head_dim),
        grid=(batch, nc),
        in_specs=[
            pl.BlockSpec((S_CHUNK, xd.shape[1]), lambda b, c: (b * nc + c, 0)),
            pl.BlockSpec((S_CHUNK, inner), lambda b, c: (b * nc + c, 0)),
            const(conv_w.shape), const(conv_b.shape), const(dtb_pad.shape), const(alog_pad.shape),
            const(dskip_exp.shape), const(norm_g.shape), const(tri.shape),
        ],
        out_specs=[
            pl.BlockSpec((S_CHUNK, inner), lambda b, c: (b * nc + c, 0)),
            pl.BlockSpec((1, pairs, LANES, state), lambda b, c: (b, 0, 0, 0)),
        ],
        out_shape=[jax.ShapeDtypeStruct((t, inner), BF16),
                   jax.ShapeDtypeStruct((batch, pairs, LANES, state), F32)],
        scratch_shapes=[pltpu.VMEM((SUBLANES, conv_dim), F32), pltpu.VMEM((pairs, LANES, state), F32),
                        pltpu.VMEM((S_CHUNK, inner), F32)],
        compiler_params=_params(("parallel", "arbitrary")),
        name="ssd_prompt",
    )(xd, z, conv_w, conv_b, dtb_pad, alog_pad, dskip_exp, norm_g, tri)


def _ssd_step_prep_kernel(xd_ref, cs_ref, cw_ref, cb_ref, dtb_ref, alog_ref, exp_ref,
                          xbc_ref, xdt_t_ref, da_ref, *, inner, conv_dim):
    n_tap = cw_ref.shape[0]
    out = cb_ref[...] + cw_ref[n_tap - 1:n_tap, :] * xd_ref[:, 0:conv_dim]
    for j in range(n_tap - 1):
        out = out + cw_ref[j:j + 1, :] * cs_ref[j]
    xbc = _silu(out)
    xbc_ref[...] = xbc
    dtv = _softplus(xd_ref[:, conv_dim:conv_dim + LANES] + dtb_ref[...])
    da_ref[...] = jnp.exp(dtv * (-jnp.exp(alog_ref[...])))
    dt_exp = _dot3(dtv, exp_ref[...])
    xdt_t_ref[...] = (xbc[:, 0:inner] * dt_exp).T


def _ssd_step_prep(xd, conv_state_t, conv_w, conv_b, dtb_pad, alog_pad, expand, inner, conv_dim):
    bs = xd.shape[0]
    return pl.pallas_call(
        functools.partial(_ssd_step_prep_kernel, inner=inner, conv_dim=conv_dim),
        out_shape=[jax.ShapeDtypeStruct((bs, conv_dim), F32),
                   jax.ShapeDtypeStruct((inner, bs), F32),
                   jax.ShapeDtypeStruct((bs, LANES), F32)],
        compiler_params=pltpu.CompilerParams(vmem_limit_bytes=VMEM_LIMIT),
        name="ssd_step_prep",
    )(xd, conv_state_t, conv_w, conv_b, dtb_pad, alog_pad, expand)


def _ssd_step_kernel(da_ref, h_ref, xdt_t_ref, bc_ref, hn_ref, yt_ref, *, state, head_dim):
    b = pl.program_id(0)
    bs = bc_ref.shape[0]
    rows = h_ref.shape[1]
    gr = rows // S_GROUPS

    @pl.when(b == 0)
    def _():
        yt_ref[...] = jnp.zeros(yt_ref.shape, F32)

    sel = lax.broadcasted_iota(I32, (bs, state), 0) == b
    for g in range(S_GROUPS):
        b_sel = jnp.where(sel, bc_ref[:, g * state:(g + 1) * state], 0.0)
        c_sel = jnp.where(sel, bc_ref[:, (S_GROUPS + g) * state:(S_GROUPS + g + 1) * state], 0.0)
        xt = xdt_t_ref[g * gr:(g + 1) * gr, :]
        x_hi = xt.astype(BF16)
        x_lo = (xt - x_hi.astype(F32)).astype(BF16)
        b_hi = b_sel.astype(BF16)
        b_lo = (b_sel - b_hi.astype(F32)).astype(BF16)
        upd = _dot(x_hi, b_hi) + _dot(x_lo, b_hi) + _dot(x_hi, b_lo)
        hn_g = []
        for hh in range(gr // head_dim):
            r0 = g * gr + hh * head_dim
            da = da_ref[b * (rows // head_dim) + r0 // head_dim]
            hn = da * h_ref[0, r0:r0 + head_dim, :] + upd[hh * head_dim:(hh + 1) * head_dim, :]
            hn_ref[0, r0:r0 + head_dim, :] = hn
            hn_g.append(hn.astype(BF16))
        hn_g = jnp.concatenate(hn_g, axis=0)
        yt_ref[g * gr:(g + 1) * gr, :] += _dot_nt(hn_g, c_sel.astype(BF16))


def _ssd_step(da, h_all, layer, xdt_t, bc, state, head_dim):
    rows = h_all.shape[1]
    bs = bc.shape[0]
    h = jax.ShapeDtypeStruct((bs, rows, state), F32)
    return pl.pallas_call(
        functools.partial(_ssd_step_kernel, state=state, head_dim=head_dim),
        grid=(bs,),
        in_specs=[
            pl.BlockSpec(memory_space=pltpu.SMEM),
            pl.BlockSpec((1, rows, state), lambda b: (layer * bs + b, 0, 0)),
            pl.BlockSpec(xdt_t.shape, lambda b: (0, 0)),
            pl.BlockSpec(bc.shape, lambda b: (0, 0)),
        ],
        out_specs=[
            pl.BlockSpec((1, rows, state), lambda b: (b, 0, 0)),
            pl.BlockSpec((rows, bs), lambda b: (0, 0)),
        ],
        out_shape=[jax.ShapeDtypeStruct(h.shape, F32), jax.ShapeDtypeStruct((rows, bs), F32)],
        compiler_params=_params(("arbitrary",)),
        name="ssd_step",
    )(da, h_all, xdt_t, bc)


def _ssd_step_out_kernel(yt_ref, xbc_ref, z_ref, dsk_ref, ng_ref, y_ref, *, inner):
    y = yt_ref[...].T + dsk_ref[...] * xbc_ref[:, 0:inner]
    y = y * _silu(z_ref[...])
    y_ref[...] = _rms_rows(y, ng_ref[...]).astype(y_ref.dtype)


def _ssd_step_out(yt, xbc, z, dskip_exp, norm_g, inner):
    bs = xbc.shape[0]
    return pl.pallas_call(
        functools.partial(_ssd_step_out_kernel, inner=inner),
        out_shape=jax.ShapeDtypeStruct((bs, inner), BF16),
        compiler_params=pltpu.CompilerParams(vmem_limit_bytes=VMEM_LIMIT),
        name="ssd_step_out",
    )(yt, xbc, z, dskip_exp, norm_g)


def _outproj_kernel(x_ref, o_ref, y_ref, wo_ref, nx_ref, wq_ref, qn_ref, x1_ref, qx_ref, *, a_width, x_dim):
    x1 = x_ref[...] + _dot(o_ref[...], wo_ref[0:a_width, :]) + _dot(y_ref[...], wo_ref[a_width:, :])
    x1_ref[...] = x1
    hx = _rms_rows(x1, nx_ref[...]).astype(BF16)
    qx = _dot(hx, wq_ref[...])
    qx_ref[...] = _head_rms(qx, qn_ref[...], x_dim).astype(qx_ref.dtype)


def _outproj(x, o, y, w_out, norm_x, wq_x, qx_norm, tm):
    t, d = x.shape
    a_width = o.shape[1]
    x_width = wq_x.shape[1]
    row = lambda w: pl.BlockSpec((tm, w), lambda m: (m, 0))
    const = lambda a: pl.BlockSpec(a.shape, lambda m: (0, 0))
    return pl.pallas_call(
        functools.partial(_outproj_kernel, a_width=a_width, x_dim=qx_norm.shape[1]),
        grid=(t // tm,),
        in_specs=[row(d), row(a_width), row(y.shape[1]), const(w_out), const(norm_x), const(wq_x),
                  const(qx_norm)],
        out_specs=[row(d), row(x_width)],
        out_shape=[jax.ShapeDtypeStruct((t, d), F32), jax.ShapeDtypeStruct((t, x_width), BF16)],
        compiler_params=_params(("parallel",)),
        name="out_proj",
    )(x, o, y, w_out, norm_x, wq_x, qx_norm)


def _memkv_kernel(mem_ref, g_ref, w_ref, kn_ref, mk_ref, mv_ref, *, x_dim):
    m = _rms_rows(mem_ref[...], g_ref[...]).astype(BF16)
    kv = _dot(m, w_ref[...])
    xw = mk_ref.shape[1]
    mk_ref[...] = _head_rms(kv[:, 0:xw], kn_ref[...], x_dim)
    mv_ref[...] = kv[:, xw:]


def _memkv(mem, norm_mem, w_kv, kx_norm, tm):
    t, d = mem.shape
    xw = w_kv.shape[1] // 2
    const = lambda a: pl.BlockSpec(a.shape, lambda m: (0, 0))
    return pl.pallas_call(
        functools.partial(_memkv_kernel, x_dim=kx_norm.shape[1]),
        grid=(t // tm,),
        in_specs=[pl.BlockSpec((tm, d), lambda m: (m, 0)), const(norm_mem), const(w_kv), const(kx_norm)],
        out_specs=[pl.BlockSpec((tm, xw), lambda m: (m, 0))] * 2,
        out_shape=[jax.ShapeDtypeStruct((t, xw), F32)] * 2,
        compiler_params=_params(("parallel",)),
        name="memory_kv",
    )(mem, norm_mem, w_kv, kx_norm)


def _xattn_heads(q, mk, mv, x_dim):
    scale = x_dim ** -0.5
    outs = []
    for h in range(q.shape[1] // x_dim):
        sl = slice(h * x_dim, (h + 1) * x_dim)
        s = _dot_nt(q[:, sl], mk[:, sl].astype(BF16)) * scale
        s = s - jnp.max(s, axis=-1, keepdims=True)
        e = jnp.exp(s)
        p = e / jnp.sum(e, axis=-1, keepdims=True)
        outs.append(_dot(p.astype(BF16), mv[:, sl].astype(BF16)))
    return jnp.concatenate(outs, axis=1)


def _xattn_prompt_kernel(q_ref, mk_ref, mv_ref, o_ref, *, x_dim):
    o_ref[...] = _xattn_heads(q_ref[...], mk_ref[...], mv_ref[...], x_dim).astype(o_ref.dtype)


def _xattn_prompt(qx, mk, mv, batch, seq, n_mem, x_dim, tq):
    t, xw = qx.shape
    nq = seq // tq
    return pl.pallas_call(
        functools.partial(_xattn_prompt_kernel, x_dim=x_dim),
        grid=(batch, nq),
        in_specs=[pl.BlockSpec((tq, xw), lambda b, i: (b * nq + i, 0)),
                  pl.BlockSpec((n_mem, xw), lambda b, i: (b, 0)),
                  pl.BlockSpec((n_mem, xw), lambda b, i: (b, 0))],
        out_specs=pl.BlockSpec((tq, xw), lambda b, i: (b * nq + i, 0)),
        out_shape=jax.ShapeDtypeStruct((t, xw), BF16),
        compiler_params=_params(("parallel", "parallel")),
        name="xattn_prompt",
    )(qx, mk, mv)


def _xattn_decode_kernel(q_ref, mk_ref, mv_ref, o_ref, *, x_dim):
    scale = x_dim ** -0.5
    for i in range(q_ref.shape[0]):
        s = jnp.sum(mk_ref[0, i] * q_ref[i][None], axis=-1, keepdims=True) * scale
        e = jnp.exp(s - jnp.max(s, axis=0, keepdims=True))
        p = e / jnp.sum(e, axis=0, keepdims=True)
        o_ref[i] = jnp.sum(p * mv_ref[0, i], axis=0)


def _xattn_decode(qx, mem_k, mem_v, layer, per_step):
    bs, xw = qx.shape
    n_mem, heads, x_dim = mem_k.shape[2:]
    q3 = qx.astype(F32).reshape(bs, heads, x_dim)
    mem_spec = pl.BlockSpec((1, per_step, n_mem, heads, x_dim), lambda b: (layer, b, 0, 0, 0))
    out = pl.pallas_call(
        functools.partial(_xattn_decode_kernel, x_dim=x_dim),
        grid=(bs // per_step,),
        in_specs=[pl.BlockSpec((per_step, heads, x_dim), lambda b: (b, 0, 0)), mem_spec, mem_spec],
        out_specs=pl.BlockSpec((per_step, heads, x_dim), lambda b: (b, 0, 0)),
        out_shape=jax.ShapeDtypeStruct((bs, heads, x_dim), F32),
        compiler_params=_params(("parallel",)),
        name="xattn_decode",
    )(q3, mem_k, mem_v)
    return out.reshape(bs, xw).astype(BF16)


def _xproj_kernel(x1_ref, ox_ref, wo_ref, nf_ref, wq_ref, x2_ref, hf_ref, pq_ref):
    x2 = x1_ref[...] + _dot(ox_ref[...], wo_ref[...])
    x2_ref[...] = x2
    hf = _rms_rows(x2, nf_ref[...]).astype(BF16)
    hf_ref[...] = hf
    pq_ref[...] = _dot(hf, wq_ref[...])


def _xproj(x1, ox, wo_x, norm_ffn, peer_wq, tm):
    t, d = x1.shape
    row = lambda w: pl.BlockSpec((tm, w), lambda m: (m, 0))
    const = lambda a: pl.BlockSpec(a.shape, lambda m: (0, 0))
    return pl.pallas_call(
        _xproj_kernel,
        grid=(t // tm,),
        in_specs=[row(d), row(ox.shape[1]), const(wo_x), const(norm_ffn), const(peer_wq)],
        out_specs=[row(d), row(d), row(peer_wq.shape[1])],
        out_shape=[jax.ShapeDtypeStruct((t, d), F32), jax.ShapeDtypeStruct((t, d), BF16),
                   jax.ShapeDtypeStruct((t, peer_wq.shape[1]), F32)],
        compiler_params=_params(("parallel",)),
        name="xattn_out_proj",
    )(x1, ox, wo_x, norm_ffn, peer_wq)


def _extract_max(s, order):
    m = jnp.max(s, axis=0, keepdims=True)
    big = jnp.float32(1e9)
    first = jnp.min(jnp.where(s == m, order, big), axis=0, keepdims=True)
    return m, first, order == first


def _cand_blocks():
    blocks = [(0, 0), (SUBLANES, 0)]
    blocks += [(0, b) for b in range(1, SUBLANES)]
    return blocks


def _route_kernel(pq_ref, keys_ref, io_ref, jo_ref, go_ref, sc_ref, i_ref, j_ref, g_ref, *, heads, nkeys):
    tt = pq_ref.shape[0]
    k = P_TOPK
    neg = jnp.float32(-jnp.inf)
    order1 = lax.broadcasted_iota(I32, (nkeys, tt), 0).astype(F32)
    pq = pq_ref[...].astype(BF16)
    for h in range(heads):
        sc_ref[h] = _dot_nt(keys_ref[h], pq[:, h * LANES:(h + 1) * LANES])

    def head_body(h, carry):
        sv, si = [], []
        for c in range(2):
            s = sc_ref[h, c * nkeys:(c + 1) * nkeys, :]
            vals, idxs = [], []
            for _ in range(k):
                m, first, hit = _extract_max(s, order1)
                vals.append(m)
                idxs.append(first)
                s = jnp.where(hit, neg, s)
            sv.append(jnp.concatenate(vals, axis=0))
            si.append(jnp.concatenate(idxs, axis=0))
        cand, ci, cj, flat = [], [], [], []
        row8 = lax.broadcasted_iota(I32, (SUBLANES, tt), 0).astype(F32)
        for a0, b in _cand_blocks():
            cand.append(sv[0][a0:a0 + SUBLANES] + sv[1][b:b + 1])
            ci.append(si[0][a0:a0 + SUBLANES])
            cj.append(jnp.broadcast_to(si[1][b:b + 1], (SUBLANES, tt)))
            flat.append((row8 + a0) * k + b)
        cand.append(sv[0][0:1] + sv[1][SUBLANES:k])
        ci.append(jnp.broadcast_to(si[0][0:1], (SUBLANES, tt)))
        cj.append(si[1][SUBLANES:k])
        flat.append(row8 + SUBLANES)
        cand, ci, cj, flat = (jnp.concatenate(v, axis=0) for v in (cand, ci, cj, flat))
        fs, fi, fj = [], [], []
        for _ in range(k):
            m, _, hit = _extract_max(cand, flat)
            fs.append(m)
            fi.append(jnp.sum(jnp.where(hit, ci, 0.0), axis=0, keepdims=True))
            fj.append(jnp.sum(jnp.where(hit, cj, 0.0), axis=0, keepdims=True))
            cand = jnp.where(hit, neg, cand)
        fs, fi, fj = (jnp.concatenate(v, axis=0) for v in (fs, fi, fj))
        e = jnp.exp(fs - fs[0:1])
        gate = e / jnp.sum(e, axis=0, keepdims=True)
        r0 = pl.multiple_of(h * k, k)
        i_ref[pl.ds(r0, k), :] = fi
        j_ref[pl.ds(r0, k), :] = fj
        g_ref[pl.ds(r0, k), :] = gate
        return carry

    lax.fori_loop(0, heads, head_body, 0)
    io_ref[...] = i_ref[...].T.astype(I32)
    jo_ref[...] = j_ref[...].T.astype(I32)
    go_ref[...] = g_ref[...].T


def _route(pq, keys_pad, tt):
    t = pq.shape[0]
    heads, two_nkeys, _ = keys_pad.shape
    nkeys = two_nkeys // 2
    hk = heads * P_TOPK
    outs = pl.pallas_call(
        functools.partial(_route_kernel, heads=heads, nkeys=nkeys),
        grid=(t // tt,),
        in_specs=[pl.BlockSpec((tt, pq.shape[1]), lambda m: (m, 0)),
                  pl.BlockSpec(keys_pad.shape, lambda m: (0, 0, 0))],
        out_specs=[pl.BlockSpec((tt, hk), lambda m: (m, 0))] * 3,
        out_shape=[jax.ShapeDtypeStruct((t, hk), I32), jax.ShapeDtypeStruct((t, hk), I32),
                   jax.ShapeDtypeStruct((t, hk), F32)],
        scratch_shapes=[pltpu.VMEM((heads, two_nkeys, tt), F32)] + [pltpu.VMEM((hk, tt), F32)] * 3,
        compiler_params=_params(("parallel",)),
        name="peer_route",
    )(pq, keys_pad)
    return outs


def _peer_act_kernel(hf_ref, u_ref, i_ref, j_ref, act_ref, *, nkeys):
    e = pl.program_id(1)

    @pl.when(e == 0)
    def _():
        act_ref[...] = jnp.zeros(act_ref.shape, F32)

    hf = hf_ref[...]
    ii = i_ref[...]
    jj = j_ref[...]
    act = act_ref[...]
    per = u_ref.shape[0] // nkeys
    for c0 in range(0, per, 2):
        s = _dot_nt(hf, u_ref[c0 * nkeys:(c0 + 2) * nkeys, :])
        for c in (c0, c0 + 1):
            picked = jnp.take_along_axis(s[:, (c - c0) * nkeys:(c - c0 + 1) * nkeys], jj, axis=1)
            act = jnp.where(ii == e * per + c, picked, act)
    act_ref[...] = act


def _peer_act(hf, u_bf16, ii, jj, nkeys, tt, eb):
    t, d = hf.shape
    hk = ii.shape[1]
    return pl.pallas_call(
        functools.partial(_peer_act_kernel, nkeys=nkeys),
        grid=(t // tt, u_bf16.shape[0] // eb),
        in_specs=[pl.BlockSpec((tt, d), lambda m, e: (m, 0)),
                  pl.BlockSpec((eb, d), lambda m, e: (e, 0)),
                  pl.BlockSpec((tt, hk), lambda m, e: (m, 0)),
                  pl.BlockSpec((tt, hk), lambda m, e: (m, 0))],
        out_specs=pl.BlockSpec((tt, hk), lambda m, e: (m, 0)),
        out_shape=jax.ShapeDtypeStruct((t, hk), F32),
        compiler_params=_params(("parallel", "arbitrary")),
        name="peer_act",
    )(hf, u_bf16, ii, jj)


def _peer_out_kernel(act_ref, g_ref, i_ref, j_ref, x_ref, v_ref, y_ref, wd_ref, w_ref, *, nkeys, pitch):
    e = pl.program_id(1)
    tt = act_ref.shape[0]
    hk = act_ref.shape[1]
    per = v_ref.shape[0] // nkeys

    @pl.when(e == 0)
    def _():
        a = act_ref[...]
        gelu = 0.5 * a * (1.0 + lax.erf(a * (2.0 ** -0.5)))
        w_ref[...] = g_ref[...] * gelu
        y_ref[...] = x_ref[...]
        sub = lax.broadcasted_iota(I32, (nkeys, hk), 0)

        def group(gidx, carry):
            t0 = pl.multiple_of(gidx * SUBLANES, SUBLANES)
            w8 = w_ref[pl.ds(t0, SUBLANES), :]
            i8 = i_ref[pl.ds(t0, SUBLANES), :]
            j8 = j_ref[pl.ds(t0, SUBLANES), :]
            for r in range(SUBLANES):
                wi = jnp.where(sub == i8[r:r + 1], w8[r:r + 1], 0.0).astype(BF16)
                oj = jnp.where(sub == j8[r:r + 1], 1.0, 0.0).astype(BF16)
                row0 = pl.multiple_of((t0 + r) * pitch, SUBLANES)
                wd_ref[pl.ds(row0, nkeys), :] = _dot_nt(wi, oj)
            return carry

        lax.fori_loop(0, tt // SUBLANES, group, 0)

    lhs = jnp.concatenate([wd_ref[pl.ds(e * per + c, tt, stride=pitch), :].astype(BF16)
                           for c in range(per)], axis=1)
    y_ref[...] += _dot(lhs, v_ref[...])


def _peer_out(act, gate, ii, jj, x2, v_bf16, nkeys, tt, eb):
    t, d = x2.shape
    hk = act.shape[1]
    pitch = nkeys + SUBLANES
    tok = lambda w: pl.BlockSpec((tt, w), lambda m, e: (m, 0))
    return pl.pallas_call(
        functools.partial(_peer_out_kernel, nkeys=nkeys, pitch=pitch),
        grid=(t // tt, v_bf16.shape[0] // eb),
        in_specs=[tok(hk), tok(hk), tok(hk), tok(hk), tok(d),
                  pl.BlockSpec((eb, d), lambda m, e: (e, 0))],
        out_specs=tok(d),
        out_shape=jax.ShapeDtypeStruct((t, d), F32),
        scratch_shapes=[pltpu.VMEM((tt * pitch, nkeys), F32), pltpu.VMEM((tt, hk), F32)],
        compiler_params=_params(("parallel", "arbitrary")),
        name="peer_out",
    )(act, gate, ii, jj, x2, v_bf16)


def _tile(n, pref):
    return pref if n % pref == 0 else n


def _layer_consts(lw, a_dim, heads_s, head_dim, inner, conv_dim):
    d = lw['w_in'].shape[0]
    a_width = lw['w_out'].shape[0] - inner
    tn = a_width
    cols = lw['w_in'].shape[1]
    assert cols - 4 * tn - conv_dim == heads_s <= LANES
    n_pad = 4 * tn + -(-(conv_dim + LANES) // tn) * tn
    w_pad = jnp.zeros((d, n_pad), BF16).at[:, :cols].set(lw['w_in'].astype(BF16))
    reps = a_width // a_dim
    colgain = jnp.concatenate([jnp.tile(lw['q_norm'], reps) * (a_dim ** -0.5),
                               jnp.tile(lw['k_norm'], reps)])[None, :]
    gid = jnp.arange(LANES) // a_dim
    bd = (gid[:, None] == gid[None, :]).astype(BF16)
    pad_h = lambda v: jnp.zeros((1, LANES), F32).at[0, :heads_s].set(v)
    tri = (jnp.arange(S_CHUNK)[None, :] <= jnp.arange(S_CHUNK)[:, None]).astype(BF16)
    expand = (jnp.arange(LANES)[:, None] == (jnp.arange(inner) // head_dim)[None, :]).astype(BF16)
    keys = lw['peer_keys']
    _, p_heads, nkeys, half = keys.shape
    keys_pad = jnp.zeros((p_heads, 2 * nkeys, 2 * half), BF16)
    keys_pad = keys_pad.at[:, :nkeys, :half].set(keys[0].astype(BF16))
    keys_pad = keys_pad.at[:, nkeys:, half:].set(keys[1].astype(BF16))
    return dict(
        w_pad=w_pad, colgain=colgain, bd=bd, tri=tri, expand=expand, keys_pad=keys_pad,
        norm_mix=lw['norm_mix'][None, :], subln=lw['subln'][None, :],
        lams=tuple(lw[n][None, :] for n in ('lambda_q1', 'lambda_k1', 'lambda_q2', 'lambda_k2')),
        conv_w=lw['conv_w'], conv_b=lw['conv_b'][None, :],
        dtb=pad_h(lw['dt_bias']), alog=pad_h(lw['a_log']),
        dskip=jnp.repeat(lw['d_skip'], head_dim)[None, :], ssm_norm=lw['ssm_norm'][None, :],
        w_out=lw['w_out'].astype(BF16), norm_x=lw['norm_x'][None, :], wq_x=lw['wq_x'].astype(BF16),
        qx_norm=lw['qx_norm'][None, :], kx_norm=lw['kx_norm'][None, :], norm_mem=lw['norm_mem'][None, :],
        w_kv=jnp.concatenate([lw['wk_x'], lw['wv_x']], axis=1).astype(BF16),
        wo_x=lw['wo_x'].astype(BF16), norm_ffn=lw['norm_ffn'][None, :],
        peer_wq=lw['peer_wq'].astype(BF16), peer_u=lw['peer_u'].astype(BF16),
        peer_v=lw['peer_v'].astype(BF16), nkeys=nkeys,
    )


def _peer(c, x2, hf, pq, tt_act, tt_out):
    t = x2.shape[0]
    nkeys = c['nkeys']
    ii, jj, gate = _route(pq, c['keys_pad'], LANES)
    act = _peer_act(hf, c['peer_u'], ii, jj, nkeys, _tile(t, tt_act), PEER_ACT_BLOCKS * nkeys)
    return _peer_out(act, gate, ii, jj, x2, c['peer_v'], nkeys, _tile(t, tt_out), PEER_OUT_BLOCKS * nkeys)


def kernel(x_prompt, x_sample, cache_k, cache_v, cache_mem_k, cache_mem_v, state_conv, state_ssm, page_table, mem_prompt, norm_mix, w_in, q_norm, k_norm, lambda_q1, lambda_k1, lambda_q2, lambda_k2, subln, conv_w, conv_b, dt_bias, a_log, d_skip, ssm_norm, w_out, norm_x, norm_mem, wq_x, wk_x, wv_x, qx_norm, kx_norm, wo_x, norm_ffn, peer_wq, peer_keys, peer_u, peer_v):
    weights = dict(norm_mix=norm_mix, w_in=w_in, q_norm=q_norm, k_norm=k_norm, lambda_q1=lambda_q1,
                   lambda_k1=lambda_k1, lambda_q2=lambda_q2, lambda_k2=lambda_k2, subln=subln,
                   conv_w=conv_w, conv_b=conv_b, dt_bias=dt_bias, a_log=a_log, d_skip=d_skip,
                   ssm_norm=ssm_norm, w_out=w_out, norm_x=norm_x, norm_mem=norm_mem, wq_x=wq_x,
                   wk_x=wk_x, wv_x=wv_x, qx_norm=qx_norm, kx_norm=kx_norm, wo_x=wo_x,
                   norm_ffn=norm_ffn, peer_wq=peer_wq, peer_keys=peer_keys, peer_u=peer_u, peer_v=peer_v)
    depth = w_in.shape[0]
    bp, seq, d = x_prompt.shape
    bs, dec_seq, _ = x_sample.shape
    assert dec_seq == 1
    a_dim = q_norm.shape[-1]
    a_heads = cache_k.shape[3]
    a_width = a_heads * 2 * a_dim
    s_heads, head_dim, state = state_ssm.shape[2:]
    inner = s_heads * head_dim
    conv_dim = state_conv.shape[-1]
    n_mem, x_heads, x_dim = cache_mem_k.shape[2:]
    x_width = x_heads * x_dim
    assert a_width == inner and 2 * a_dim == LANES and 2 * head_dim == LANES and state == LANES

    yp = x_prompt.reshape(bp * seq, d)
    ys = x_sample.reshape(bs, d)
    outs = {n: [] for n in ('kp', 'vp', 'cp', 'sp', 'mkp', 'mvp', 'ks', 'vs', 'cs', 'ss')}
    for l in range(depth):
        lw = {n: w[l] for n, w in weights.items()}
        c = _layer_consts(lw, a_dim, s_heads, head_dim, inner, conv_dim)
        lam0 = 0.8 - 0.6 * math.exp(-0.3 * l)

        mk, mv = _memkv(mem_prompt.reshape(bp * n_mem, d), c['norm_mem'], c['w_kv'], c['kx_norm'],
                        _tile(bp * n_mem, 256))
        q, k, v, z, xd = _inproj(yp, c['norm_mix'], c['w_pad'], c['colgain'], c['bd'], a_dim,
                                 _tile(bp * seq, 512))
        o = _attn_prompt(q, k, v, c['subln'].T, c['lams'], bp, seq, a_dim, lam0, _tile(seq, 256))
        ym, st = _ssd_prompt(xd, z, c['conv_w'], c['conv_b'], c['dtb'], c['alog'], c['dskip'],
                             c['ssm_norm'], c['tri'], bp, seq, inner, state, conv_dim, head_dim)
        x1, qx = _outproj(yp, o, ym, c['w_out'], c['norm_x'], c['wq_x'], c['qx_norm'], _tile(bp * seq, 256))
        ox = _xattn_prompt(qx, mk, mv, bp, seq, n_mem, x_dim, _tile(seq, 256))
        x2, hf, pq = _xproj(x1, ox, c['wo_x'], c['norm_ffn'], c['peer_wq'], _tile(bp * seq, 256))
        yp = _peer(c, x2, hf, pq, 512, 256)
        outs['kp'].append(k.reshape(bp, seq, a_heads, 2 * a_dim))
        outs['vp'].append(v.reshape(bp, seq, a_heads, 2 * a_dim))
        outs['cp'].append(xd.reshape(bp, seq, -1)[:, seq - (conv_w.shape[1] - 1):, :conv_dim])
        outs['sp'].append(st.reshape(bp, s_heads, head_dim, state))
        outs['mkp'].append(mk.reshape(bp, n_mem, x_heads, x_dim))
        outs['mvp'].append(mv.reshape(bp, n_mem, x_heads, x_dim))

        q, k, v, z, xd = _inproj(ys, c['norm_mix'], c['w_pad'], c['colgain'], c['bd'], a_dim, bs)
        o = _attn_decode(q, k, v, cache_k, cache_v, l, page_table, c['subln'], c['lams'], a_dim,
                         lam0).astype(BF16)
        conv_prev = state_conv[l]
        xbc, xdt_t, da = _ssd_step_prep(xd, jnp.swapaxes(conv_prev, 0, 1), c['conv_w'], c['conv_b'],
                                        c['dtb'], c['alog'], c['expand'], inner, conv_dim)
        hn, yt = _ssd_step(da[:, :s_heads].reshape(-1), state_ssm.reshape(depth * bs, inner, state), l,
                           xdt_t, xbc[:, inner:], state, head_dim)
        ym = _ssd_step_out(yt, xbc, z, c['dskip'], c['ssm_norm'], inner)
        x1, qx = _outproj(ys, o, ym, c['w_out'], c['norm_x'], c['wq_x'], c['qx_norm'], bs)
        ox = _xattn_decode(qx, cache_mem_k, cache_mem_v, l, 4)
        x2, hf, pq = _xproj(x1, ox, c['wo_x'], c['norm_ffn'], c['peer_wq'], bs)
        ys = _peer(c, x2, hf, pq, bs, bs)
        outs['ks'].append(k.reshape(bs, 1, a_heads, 2 * a_dim))
        outs['vs'].append(v.reshape(bs, 1, a_heads, 2 * a_dim))
        outs['cs'].append(jnp.concatenate([conv_prev[:, 1:], xd[:, None, :conv_dim]], axis=1))
        outs['ss'].append(hn.reshape(bs, s_heads, head_dim, state))

    st = lambda n: jnp.stack(outs[n]) if depth > 1 else outs[n][0][None]
    return (yp.reshape(bp, seq, d), ys.reshape(bs, 1, d), st('kp'), st('vp'), st('cp'), st('sp'),
            st('mkp'), st('mvp'), st('ks'), st('vs'), st('cs'), st('ss'))
```

```python
import functools
import math

import jax
import jax.numpy as jnp
from jax import lax
from jax.experimental import pallas as pl
from jax.experimental.pallas import tpu as pltpu

F32 = jnp.float32
BF16 = jnp.bfloat16
I32 = jnp.int32

EPS = 1e-6
LANES = 128
SUBLANES = 8
S_GROUPS = 2
S_CHUNK = 128
P_TOPK = 16
PAGES_PER_STEP = 8
PEER_ACT_BLOCKS = 8
PEER_OUT_BLOCKS = 16
ROUTE_PARTS = 4
VMEM_LIMIT = 56 << 20

NT_DIMS = (((1,), (1,)), ((), ()))


def _params(sem):
    return pltpu.CompilerParams(dimension_semantics=sem, vmem_limit_bytes=VMEM_LIMIT)


def _dot(a, b):
    return jnp.dot(a, b, preferred_element_type=F32)


def _dot_nt(a, b):
    return lax.dot_general(a, b, NT_DIMS, preferred_element_type=F32)


def _split3(x):
    hi = x.astype(BF16)
    r1 = x - hi.astype(F32)
    mid = r1.astype(BF16)
    lo = (r1 - mid.astype(F32)).astype(BF16)
    return hi, mid, lo


def _dot3(x, m_bf16):
    hi, mid, lo = _split3(x)
    return _dot(hi, m_bf16) + _dot(mid, m_bf16) + _dot(lo, m_bf16)


def _rms_rows(x, gain):
    ms = jnp.mean(x * x, axis=-1, keepdims=True)
    return x * lax.rsqrt(ms + EPS) * gain


def _head_rms(x, gain, width):
    outs = []
    for h in range(x.shape[1] // width):
        c = x[:, h * width:(h + 1) * width]
        outs.append(_rms_rows(c, gain))
    return outs[0] if len(outs) == 1 else jnp.concatenate(outs, axis=1)


def _silu(x):
    return x * (1.0 / (1.0 + jnp.exp(-x)))


def _softplus(x):
    return jnp.maximum(x, 0.0) + jnp.log1p(jnp.exp(-jnp.abs(x)))


def _inproj_kernel(x_ref, g_ref, w_ref, cg_ref, bd_ref, q_ref, k_ref, v_ref, z_ref, xd_ref,
                   xn_ref, *, group):
    n = pl.program_id(1)

    @pl.when(n == 0)
    def _():
        xn_ref[...] = _rms_rows(x_ref[...], g_ref[...]).astype(BF16)

    acc = _dot(xn_ref[...], w_ref[...])
    tn = acc.shape[1]

    def group_norm(dst_ref):
        for c in range(tn // LANES):
            a = acc[:, c * LANES:(c + 1) * LANES]
            sq = a * a
            hi = sq.astype(BF16)
            lo = (sq - hi.astype(F32)).astype(BF16)
            ss = _dot(hi, bd_ref[...]) + _dot(lo, bd_ref[...])
            y = a * lax.rsqrt(ss * (1.0 / group) + EPS) * cg_ref[:, c * LANES:(c + 1) * LANES]
            dst_ref[:, c * LANES:(c + 1) * LANES] = y.astype(dst_ref.dtype)

    @pl.when(n == 0)
    def _():
        group_norm(q_ref)

    @pl.when(n == 1)
    def _():
        group_norm(k_ref)

    @pl.when(n == 2)
    def _():
        v_ref[...] = acc

    @pl.when(n == 3)
    def _():
        z_ref[...] = acc

    @pl.when(n >= 4)
    def _():
        xd_ref[...] = acc


def _inproj(x, gain, w_pad, colgain, bd, a_dim, tm):
    t, d = x.shape
    n_pad = w_pad.shape[1]
    tn = colgain.shape[1] // 2
    nt = n_pad // tn
    nx = nt - 4
    grid = (t // tm, nt)
    row = lambda m, n: (m, 0)
    return pl.pallas_call(
        functools.partial(_inproj_kernel, group=a_dim),
        grid=grid,
        in_specs=[
            pl.BlockSpec((tm, d), row),
            pl.BlockSpec((1, d), lambda m, n: (0, 0)),
            pl.BlockSpec((d, tn), lambda m, n: (0, n)),
            pl.BlockSpec((1, tn), lambda m, n: (0, jnp.minimum(n, 1))),
            pl.BlockSpec((LANES, LANES), lambda m, n: (0, 0)),
        ],
        out_specs=[
            pl.BlockSpec((tm, tn), row),
            pl.BlockSpec((tm, tn), row),
            pl.BlockSpec((tm, tn), row),
            pl.BlockSpec((tm, tn), row),
            pl.BlockSpec((tm, tn), lambda m, n: (m, jnp.clip(n - 4, 0, nx - 1))),
        ],
        out_shape=[
            jax.ShapeDtypeStruct((t, tn), BF16),
            jax.ShapeDtypeStruct((t, tn), F32),
            jax.ShapeDtypeStruct((t, tn), F32),
            jax.ShapeDtypeStruct((t, tn), F32),
            jax.ShapeDtypeStruct((t, nx * tn), F32),
        ],
        scratch_shapes=[pltpu.VMEM((tm, d), BF16)],
        compiler_params=_params(("parallel", "arbitrary")),
        name="in_proj",
    )(x, gain, w_pad, colgain, bd)


def _lambda(lq1, lk1, lq2, lk2, lam0):
    return (jnp.exp(jnp.sum(lq1[...] * lk1[...], axis=-1, keepdims=True))
            - jnp.exp(jnp.sum(lq2[...] * lk2[...], axis=-1, keepdims=True)) + lam0)


def _attn_prompt_kernel(q_ref, k_ref, v_ref, sub_ref, lq1, lk1, lq2, lk2, o_ref,
                        kb_ref, vt_ref, m_ref, l_ref, acc_ref, *, a_dim, lam0):
    qi = pl.program_id(2)
    tq = q_ref.shape[0]
    n_kv = vt_ref.shape[0]

    @pl.when(qi == 0)
    def _():
        kb_ref[...] = k_ref[...].astype(BF16)
        for j in range(n_kv):
            vt_ref[j] = v_ref[j * tq:(j + 1) * tq, :].T.astype(BF16)

    q = q_ref[...]
    lane = lax.broadcasted_iota(I32, q.shape, 1)
    zero = jnp.zeros_like(q)
    q2 = jnp.concatenate([jnp.where(lane < a_dim, q, zero), jnp.where(lane >= a_dim, q, zero)], axis=0)

    m_ref[...] = jnp.full(m_ref.shape, -jnp.inf, F32)
    l_ref[...] = jnp.zeros(l_ref.shape, F32)
    acc_ref[...] = jnp.zeros(acc_ref.shape, F32)

    def step(j, masked):
        off = pl.multiple_of(j * tq, tq)
        s = _dot_nt(kb_ref[pl.ds(off, tq), :], q2)
        if masked:
            r = lax.broadcasted_iota(I32, s.shape, 0)
            c = lax.broadcasted_iota(I32, s.shape, 1)
            c = jnp.where(c >= tq, c - tq, c)
            s = jnp.where(r <= c, s, -jnp.inf)
        m_old = m_ref[...]
        m_new = jnp.maximum(m_old, jnp.max(s, axis=0, keepdims=True))
        alpha = jnp.exp(m_old - m_new)
        p = jnp.exp(s - m_new)
        l_ref[...] = alpha * l_ref[...] + jnp.sum(p, axis=0, keepdims=True)
        acc_ref[...] = alpha * acc_ref[...] + _dot(vt_ref[j], p.astype(BF16))
        m_ref[...] = m_new

    def body(j, carry):
        step(j, False)
        return carry

    lax.fori_loop(0, qi, body, 0)
    step(qi, True)

    lam = _lambda(lq1, lk1, lq2, lk2, lam0)
    o1 = acc_ref[:, 0:tq] / l_ref[:, 0:tq]
    o2 = acc_ref[:, tq:2 * tq] / l_ref[:, tq:2 * tq]
    o = o1 - lam * o2
    ms = jnp.mean(o * o, axis=0, keepdims=True)
    o = o * lax.rsqrt(ms + EPS) * sub_ref[...] * (1.0 - lam0)
    o_ref[...] = o.T.astype(o_ref.dtype)


def _attn_prompt(q, k, v, subln_col, lams, batch, seq, a_dim, lam0, tq):
    t, width = q.shape
    hd = 2 * a_dim
    heads = width // hd
    nq = seq // tq
    small = pl.BlockSpec((1, a_dim), lambda b, h, i: (0, 0))
    return pl.pallas_call(
        functools.partial(_attn_prompt_kernel, a_dim=a_dim, lam0=lam0),
        grid=(batch, heads, nq),
        in_specs=[
            pl.BlockSpec((tq, hd), lambda b, h, i: (b * nq + i, h)),
            pl.BlockSpec((seq, hd), lambda b, h, i: (b, h)),
            pl.BlockSpec((seq, hd), lambda b, h, i: (b, h)),
            pl.BlockSpec((hd, 1), lambda b, h, i: (0, 0)),
            small, small, small, small,
        ],
        out_specs=pl.BlockSpec((tq, hd), lambda b, h, i: (b * nq + i, h)),
        out_shape=jax.ShapeDtypeStruct((t, width), BF16),
        scratch_shapes=[pltpu.VMEM((seq, hd), BF16), pltpu.VMEM((nq, hd, tq), BF16),
                        pltpu.VMEM((1, 2 * tq), F32), pltpu.VMEM((1, 2 * tq), F32),
                        pltpu.VMEM((hd, 2 * tq), F32)],
        compiler_params=_params(("parallel", "parallel", "arbitrary")),
        name="diff_attn_prompt",
    )(q, k, v, subln_col, *lams)


def _decode_body(p_id, n_p, q_ref, kn_ref, vn_ref, sub_ref, lq1, lk1, lq2, lk2, kp, vp,
                 o_ref, qr_ref, m_ref, l_ref, acc_ref, *, a_dim, lam0, heads):
    rows = 2 * heads
    hd = 2 * a_dim
    page = kp[0].shape[2]
    cols = page * heads

    @pl.when(p_id == 0)
    def _():
        q8 = q_ref[0]
        lane = lax.broadcasted_iota(I32, q8.shape, 1)
        qr_ref[...] = jnp.concatenate([jnp.where(lane < a_dim, q8, 0.0),
                                       jnp.where(lane >= a_dim, q8, 0.0)], axis=0).astype(BF16)
        m_ref[...] = jnp.full(m_ref.shape, -jnp.inf, F32)
        l_ref[...] = jnp.zeros(l_ref.shape, F32)
        acc_ref[...] = jnp.zeros(acc_ref.shape, F32)

    qr = qr_ref[...]
    r = lax.broadcasted_iota(I32, (rows, cols), 0)
    c = lax.broadcasted_iota(I32, (rows, cols), 1)
    same_head = (r % heads) == (c % heads)

    s = []
    for i in range(PAGES_PER_STEP):
        s.append(jnp.where(same_head, _dot_nt(qr, kp[i][0, 0].reshape(cols, hd).astype(BF16)), -jnp.inf))
        yield
    m_old = m_ref[...]
    m_new = m_old
    for si in s:
        m_new = jnp.maximum(m_new, jnp.max(si, axis=-1, keepdims=True))
    alpha = jnp.exp(m_old - m_new)
    l_new = alpha * l_ref[...]
    acc = alpha * acc_ref[...]
    for i, si in enumerate(s):
        p = jnp.exp(si - m_new)
        l_new = l_new + jnp.sum(p, axis=-1, keepdims=True)
        acc = acc + _dot(p.astype(BF16), vp[i][0, 0].reshape(cols, hd).astype(BF16))
        yield
    m_ref[...] = m_new
    l_ref[...] = l_new
    acc_ref[...] = acc

    @pl.when(p_id == n_p - 1)
    def _():
        kn = kn_ref[0].astype(BF16).astype(F32)
        vn = vn_ref[0].astype(BF16).astype(F32)
        kn2 = jnp.concatenate([kn, kn], axis=0)
        vn2 = jnp.concatenate([vn, vn], axis=0)
        s_new = jnp.sum(qr.astype(F32) * kn2, axis=-1, keepdims=True)
        m_fin = jnp.maximum(m_new, s_new)
        a_fin = jnp.exp(m_new - m_fin)
        p_new = jnp.exp(s_new - m_fin)
        l_fin = a_fin * l_new + p_new
        acc_fin = a_fin * acc + p_new.astype(BF16).astype(F32) * vn2

        lam = _lambda(lq1, lk1, lq2, lk2, lam0)
        o1 = acc_fin[0:heads, :] / l_fin[0:heads, :]
        o2 = acc_fin[heads:rows, :] / l_fin[heads:rows, :]
        o_ref[0] = _rms_rows(o1 - lam * o2, sub_ref[...]) * (1.0 - lam0)


def _conv_taps(cur, tail, w_ref, b_ref):
    n_tap = w_ref.shape[0]
    out = b_ref[...] + w_ref[n_tap - 1:n_tap, :] * cur
    row8 = lax.broadcasted_iota(I32, (SUBLANES, cur.shape[1]), 0)
    for s in range(1, n_tap):
        rolled = pltpu.roll(cur, s, 0)
        head = jnp.where(row8 < s, pltpu.roll(tail, s, 0), rolled[0:SUBLANES])
        shifted = jnp.concatenate([head, rolled[SUBLANES:]], axis=0)
        out = out + w_ref[n_tap - 1 - s:n_tap - s, :] * shifted
    return out


def _ssd_prompt_kernel(xd_ref, z_ref, cw_ref, cb_ref, dtb_ref, alog_ref, dsk_ref, ng_ref, tri_ref,
                       y_ref, st_ref, tail_ref, h_ref, yg_ref, *, inner, state, conv_dim, head_dim):
    c_id = pl.program_id(1)
    ch = xd_ref.shape[0]
    pairs = h_ref.shape[0]
    rep = (inner // head_dim) // S_GROUPS

    @pl.when(c_id == 0)
    def _():
        tail_ref[...] = jnp.zeros(tail_ref.shape, F32)
        h_ref[...] = jnp.zeros(h_ref.shape, F32)

    pre = xd_ref[:, 0:conv_dim]
    xbc = _silu(_conv_taps(pre, tail_ref[...], cw_ref, cb_ref))
    tail_ref[...] = pre[ch - SUBLANES:ch, :]

    dtv = _softplus(xd_ref[:, conv_dim:conv_dim + LANES] + dtb_ref[...])
    a = dtv * (-jnp.exp(alog_ref[...]))
    a_hi, a_mid, a_lo = _split3(a)
    tri = tri_ref[...]
    a_cum = _dot(tri, a_hi) + _dot(tri, a_mid) + _dot(tri, a_lo)
    a_cum_t = a_cum.T
    row = lax.broadcasted_iota(I32, (ch, ch), 0)
    col = lax.broadcasted_iota(I32, (ch, ch), 1)
    lane = lax.broadcasted_iota(I32, (ch, LANES), 1)
    first = lane < head_dim
    causal = col <= row

    ssq = jnp.zeros((ch, 1), F32)
    bms = [xbc[:, inner + g * state:inner + (g + 1) * state].astype(BF16) for g in range(S_GROUPS)]
    cms = [xbc[:, inner + (S_GROUPS + g) * state:inner + (S_GROUPS + g + 1) * state].astype(BF16)
           for g in range(S_GROUPS)]
    cb = [_dot_nt(cms[g], bms[g]) for g in range(S_GROUPS)]
    for pr in range(pairs):
        g = (2 * pr) // rep
        bm, cm = bms[g], cms[g]
        xs = xbc[:, pr * LANES:(pr + 1) * LANES]
        ha, hb = 2 * pr, 2 * pr + 1
        dt2 = jnp.where(first, dtv[:, ha:ha + 1], dtv[:, hb:hb + 1])
        xdt = xs * dt2
        ac2 = jnp.where(first, a_cum[:, ha:ha + 1], a_cum[:, hb:hb + 1])
        last2 = ac2[ch - 1:ch, :]
        y = jnp.zeros((ch, LANES), F32)
        for hh, keep in ((ha, first), (hb, jnp.logical_not(first))):
            seg = a_cum[:, hh:hh + 1] - a_cum_t[hh:hh + 1, :]
            lm = jnp.exp(jnp.where(causal, seg, -jnp.inf))
            y = y + _dot((cb[g] * lm).astype(BF16), jnp.where(keep, xdt, 0.0).astype(BF16))
        hprev = h_ref[pr]
        y = y + _dot_nt(cm, hprev.astype(BF16)) * jnp.exp(ac2)
        xdec = (xdt * jnp.exp(last2 - ac2)).T
        upd = _dot(xdec.astype(BF16), bm)
        rowp = lax.broadcasted_iota(I32, hprev.shape, 0)
        e_last = jnp.exp(last2)
        decay = jnp.where(rowp < head_dim, e_last[:, 0:1], e_last[:, head_dim:head_dim + 1])
        h_ref[pr] = decay * hprev + upd
        y = y + dsk_ref[:, pr * LANES:(pr + 1) * LANES] * xs
        y = y * _silu(z_ref[:, pr * LANES:(pr + 1) * LANES])
        yg_ref[:, pr * LANES:(pr + 1) * LANES] = y
        ssq = ssq + jnp.sum(y * y, axis=-1, keepdims=True)

    y_ref[...] = (yg_ref[...] * lax.rsqrt(ssq * (1.0 / inner) + EPS) * ng_ref[...]).astype(y_ref.dtype)

    @pl.when(c_id == pl.num_programs(1) - 1)
    def _():
        st_ref[0] = h_ref[...]


def _ssd_prompt(xd, z, conv_w, conv_b, dtb_pad, alog_pad, dskip_exp, norm_g, tri, batch, seq,
                inner, state, conv_dim, head_dim):
    t = xd.shape[0]
    nc = seq // S_CHUNK
    pairs = inner // LANES
    const = lambda shape: pl.BlockSpec(shape, lambda b, c: (0,) * len(shape))
    return pl.pallas_call(
        functools.partial(_ssd_prompt_kernel, inner=inner, state=state, conv_dim=conv_dim,
                          head_dim=head_dim),
        grid=(batch, nc),
        in_specs=[
            pl.BlockSpec((S_CHUNK, xd.shape[1]), lambda b, c: (b * nc + c, 0)),
            pl.BlockSpec((S_CHUNK, inner), lambda b, c: (b * nc + c, 0)),
            const(conv_w.shape), const(conv_b.shape), const(dtb_pad.shape), const(alog_pad.shape),
            const(dskip_exp.shape), const(norm_g.shape), const(tri.shape),
        ],
        out_specs=[
            pl.BlockSpec((S_CHUNK, inner), lambda b, c: (b * nc + c, 0)),
            pl.BlockSpec((1, pairs, LANES, state), lambda b, c: (b, 0, 0, 0)),
        ],
        out_shape=[jax.ShapeDtypeStruct((t, inner), BF16),
                   jax.ShapeDtypeStruct((batch, pairs, LANES, state), F32)],
        scratch_shapes=[pltpu.VMEM((SUBLANES, conv_dim), F32), pltpu.VMEM((pairs, LANES, state), F32),
                        pltpu.VMEM((S_CHUNK, inner), F32)],
        compiler_params=_params(("parallel", "arbitrary")),
        name="ssd_prompt",
    )(xd, z, conv_w, conv_b, dtb_pad, alog_pad, dskip_exp, norm_g, tri)


def _ssd_step_prep_kernel(xd_ref, cs_ref, cw_ref, cb_ref, dtb_ref, alog_ref, exp_ref,
                          xbc_ref, xdt_t_ref, da_ref, *, inner, conv_dim):
    n_tap = cw_ref.shape[0]
    out = cb_ref[...] + cw_ref[n_tap - 1:n_tap, :] * xd_ref[:, 0:conv_dim]
    for j in range(n_tap - 1):
        out = out + cw_ref[j:j + 1, :] * cs_ref[j]
    xbc = _silu(out)
    xbc_ref[...] = xbc
    dtv = _softplus(xd_ref[:, conv_dim:conv_dim + LANES] + dtb_ref[...])
    da_ref[...] = jnp.exp(dtv * (-jnp.exp(alog_ref[...])))
    dt_exp = _dot3(dtv, exp_ref[...])
    xdt_t_ref[...] = (xbc[:, 0:inner] * dt_exp).T


def _ssd_step_prep(xd, conv_state_t, conv_w, conv_b, dtb_pad, alog_pad, expand, inner, conv_dim):
    bs = xd.shape[0]
    return pl.pallas_call(
        functools.partial(_ssd_step_prep_kernel, inner=inner, conv_dim=conv_dim),
        out_shape=[jax.ShapeDtypeStruct((bs, conv_dim), F32),
                   jax.ShapeDtypeStruct((inner, bs), F32),
                   jax.ShapeDtypeStruct((bs, LANES), F32)],
        compiler_params=pltpu.CompilerParams(vmem_limit_bytes=VMEM_LIMIT),
        name="ssd_step_prep",
    )(xd, conv_state_t, conv_w, conv_b, dtb_pad, alog_pad, expand)


def _ssd_step_kernel(da_ref, h_ref, xdt_t_ref, bc_ref, hn_ref, yt_ref, *, state, head_dim):
    b = pl.program_id(0)
    bs = bc_ref.shape[0]
    rows = h_ref.shape[1]
    gr = rows // S_GROUPS

    @pl.when(b == 0)
    def _():
        yt_ref[...] = jnp.zeros(yt_ref.shape, F32)

    sel = lax.broadcasted_iota(I32, (bs, state), 0) == b
    for g in range(S_GROUPS):
        b_sel = jnp.where(sel, bc_ref[:, g * state:(g + 1) * state], 0.0)
        c_sel = jnp.where(sel, bc_ref[:, (S_GROUPS + g) * state:(S_GROUPS + g + 1) * state], 0.0)
        xt = xdt_t_ref[g * gr:(g + 1) * gr, :]
        x_hi = xt.astype(BF16)
        x_lo = (xt - x_hi.astype(F32)).astype(BF16)
        b_hi = b_sel.astype(BF16)
        b_lo = (b_sel - b_hi.astype(F32)).astype(BF16)
        upd = _dot(x_hi, b_hi) + _dot(x_lo, b_hi) + _dot(x_hi, b_lo)
        hn_g = []
        for hh in range(gr // head_dim):
            r0 = g * gr + hh * head_dim
            da = da_ref[b * (rows // head_dim) + r0 // head_dim]
            hn = da * h_ref[0, r0:r0 + head_dim, :] + upd[hh * head_dim:(hh + 1) * head_dim, :]
            hn_ref[0, r0:r0 + head_dim, :] = hn
            hn_g.append(hn.astype(BF16))
        hn_g = jnp.concatenate(hn_g, axis=0)
        yt_ref[g * gr:(g + 1) * gr, :] += _dot_nt(hn_g, c_sel.astype(BF16))


def _ssd_step(da, h_all, layer, xdt_t, bc, state, head_dim):
    rows = h_all.shape[1]
    bs = bc.shape[0]
    h = jax.ShapeDtypeStruct((bs, rows, state), F32)
    return pl.pallas_call(
        functools.partial(_ssd_step_kernel, state=state, head_dim=head_dim),
        grid=(bs,),
        in_specs=[
            pl.BlockSpec(memory_space=pltpu.SMEM),
            pl.BlockSpec((1, rows, state), lambda b: (layer * bs + b, 0, 0)),
            pl.BlockSpec(xdt_t.shape, lambda b: (0, 0)),
            pl.BlockSpec(bc.shape, lambda b: (0, 0)),
        ],
        out_specs=[
            pl.BlockSpec((1, rows, state), lambda b: (b, 0, 0)),
            pl.BlockSpec((rows, bs), lambda b: (0, 0)),
        ],
        out_shape=[jax.ShapeDtypeStruct(h.shape, F32), jax.ShapeDtypeStruct((rows, bs), F32)],
        compiler_params=_params(("arbitrary",)),
        name="ssd_step",
    )(da, h_all, xdt_t, bc)


def _ssd_step_out_kernel(yt_ref, xbc_ref, z_ref, dsk_ref, ng_ref, y_ref, *, inner):
    y = yt_ref[...].T + dsk_ref[...] * xbc_ref[:, 0:inner]
    y = y * _silu(z_ref[...])
    y_ref[...] = _rms_rows(y, ng_ref[...]).astype(y_ref.dtype)


def _ssd_step_out(yt, xbc, z, dskip_exp, norm_g, inner):
    bs = xbc.shape[0]
    return pl.pallas_call(
        functools.partial(_ssd_step_out_kernel, inner=inner),
        out_shape=jax.ShapeDtypeStruct((bs, inner), BF16),
        compiler_params=pltpu.CompilerParams(vmem_limit_bytes=VMEM_LIMIT),
        name="ssd_step_out",
    )(yt, xbc, z, dskip_exp, norm_g)


def _outproj_kernel(x_ref, o_ref, y_ref, wo_ref, nx_ref, wq_ref, qn_ref, x1_ref, qx_ref, *, a_width, x_dim):
    x1 = x_ref[...] + _dot(o_ref[...], wo_ref[0:a_width, :]) + _dot(y_ref[...], wo_ref[a_width:, :])
    x1_ref[...] = x1
    hx = _rms_rows(x1, nx_ref[...]).astype(BF16)
    qx = _dot(hx, wq_ref[...])
    qx_ref[...] = _head_rms(qx, qn_ref[...], x_dim).astype(qx_ref.dtype)


def _outproj(x, o, y, w_out, norm_x, wq_x, qx_norm, tm):
    t, d = x.shape
    a_width = o.shape[1]
    x_width = wq_x.shape[1]
    row = lambda w: pl.BlockSpec((tm, w), lambda m: (m, 0))
    const = lambda a: pl.BlockSpec(a.shape, lambda m: (0, 0))
    return pl.pallas_call(
        functools.partial(_outproj_kernel, a_width=a_width, x_dim=qx_norm.shape[1]),
        grid=(t // tm,),
        in_specs=[row(d), row(a_width), row(y.shape[1]), const(w_out), const(norm_x), const(wq_x),
                  const(qx_norm)],
        out_specs=[row(d), row(x_width)],
        out_shape=[jax.ShapeDtypeStruct((t, d), F32), jax.ShapeDtypeStruct((t, x_width), BF16)],
        compiler_params=_params(("parallel",)),
        name="out_proj",
    )(x, o, y, w_out, norm_x, wq_x, qx_norm)


def _memkv_kernel(mem_ref, g_ref, w_ref, kn_ref, mk_ref, mv_ref, *, x_dim):
    m = _rms_rows(mem_ref[...], g_ref[...]).astype(BF16)
    kv = _dot(m, w_ref[...])
    xw = mk_ref.shape[1]
    mk_ref[...] = _head_rms(kv[:, 0:xw], kn_ref[...], x_dim)
    mv_ref[...] = kv[:, xw:]


def _memkv(mem, norm_mem, w_kv, kx_norm, tm):
    t, d = mem.shape
    xw = w_kv.shape[1] // 2
    const = lambda a: pl.BlockSpec(a.shape, lambda m: (0, 0))
    return pl.pallas_call(
        functools.partial(_memkv_kernel, x_dim=kx_norm.shape[1]),
        grid=(t // tm,),
        in_specs=[pl.BlockSpec((tm, d), lambda m: (m, 0)), const(norm_mem), const(w_kv), const(kx_norm)],
        out_specs=[pl.BlockSpec((tm, xw), lambda m: (m, 0))] * 2,
        out_shape=[jax.ShapeDtypeStruct((t, xw), F32)] * 2,
        compiler_params=_params(("parallel",)),
        name="memory_kv",
    )(mem, norm_mem, w_kv, kx_norm)


def _xattn_heads(q, mk, mv, x_dim):
    scale = x_dim ** -0.5
    outs = []
    for h in range(q.shape[1] // x_dim):
        sl = slice(h * x_dim, (h + 1) * x_dim)
        s = _dot_nt(q[:, sl], mk[:, sl].astype(BF16)) * scale
        s = s - jnp.max(s, axis=-1, keepdims=True)
        e = jnp.exp(s)
        p = e / jnp.sum(e, axis=-1, keepdims=True)
        outs.append(_dot(p.astype(BF16), mv[:, sl].astype(BF16)))
    return jnp.concatenate(outs, axis=1)


def _xattn_prompt_kernel(q_ref, mk_ref, mv_ref, o_ref, *, x_dim):
    o_ref[...] = _xattn_heads(q_ref[...], mk_ref[...], mv_ref[...], x_dim).astype(o_ref.dtype)


def _xattn_prompt(qx, mk, mv, batch, seq, n_mem, x_dim, tq):
    t, xw = qx.shape
    nq = seq // tq
    return pl.pallas_call(
        functools.partial(_xattn_prompt_kernel, x_dim=x_dim),
        grid=(batch, nq),
        in_specs=[pl.BlockSpec((tq, xw), lambda b, i: (b * nq + i, 0)),
                  pl.BlockSpec((n_mem, xw), lambda b, i: (b, 0)),
                  pl.BlockSpec((n_mem, xw), lambda b, i: (b, 0))],
        out_specs=pl.BlockSpec((tq, xw), lambda b, i: (b * nq + i, 0)),
        out_shape=jax.ShapeDtypeStruct((t, xw), BF16),
        compiler_params=_params(("parallel", "parallel")),
        name="xattn_prompt",
    )(qx, mk, mv)


def _xattn_decode_kernel(q_ref, mk_ref, mv_ref, o_ref, *, x_dim):
    scale = x_dim ** -0.5
    for i in range(q_ref.shape[0]):
        s = jnp.sum(mk_ref[0, i] * q_ref[i][None], axis=-1, keepdims=True) * scale
        e = jnp.exp(s - jnp.max(s, axis=0, keepdims=True))
        p = e / jnp.sum(e, axis=0, keepdims=True)
        o_ref[i] = jnp.sum(p * mv_ref[0, i], axis=0)


def _xattn_decode(qx, mem_k, mem_v, layer, per_step):
    bs, xw = qx.shape
    n_mem, heads, x_dim = mem_k.shape[2:]
    q3 = qx.astype(F32).reshape(bs, heads, x_dim)
    mem_spec = pl.BlockSpec((1, per_step, n_mem, heads, x_dim), lambda b: (layer, b, 0, 0, 0))
    out = pl.pallas_call(
        functools.partial(_xattn_decode_kernel, x_dim=x_dim),
        grid=(bs // per_step,),
        in_specs=[pl.BlockSpec((per_step, heads, x_dim), lambda b: (b, 0, 0)), mem_spec, mem_spec],
        out_specs=pl.BlockSpec((per_step, heads, x_dim), lambda b: (b, 0, 0)),
        out_shape=jax.ShapeDtypeStruct((bs, heads, x_dim), F32),
        compiler_params=_params(("parallel",)),
        name="xattn_decode",
    )(q3, mem_k, mem_v)
    return out.reshape(bs, xw).astype(BF16)


def _xproj_kernel(x1_ref, ox_ref, wo_ref, nf_ref, wq_ref, x2_ref, hf_ref, pq_ref):
    x2 = x1_ref[...] + _dot(ox_ref[...], wo_ref[...])
    x2_ref[...] = x2
    hf = _rms_rows(x2, nf_ref[...]).astype(BF16)
    hf_ref[...] = hf
    pq_ref[...] = _dot(hf, wq_ref[...])


def _xproj(x1, ox, wo_x, norm_ffn, peer_wq, tm):
    t, d = x1.shape
    row = lambda w: pl.BlockSpec((tm, w), lambda m: (m, 0))
    const = lambda a: pl.BlockSpec(a.shape, lambda m: (0, 0))
    return pl.pallas_call(
        _xproj_kernel,
        grid=(t // tm,),
        in_specs=[row(d), row(ox.shape[1]), const(wo_x), const(norm_ffn), const(peer_wq)],
        out_specs=[row(d), row(d), row(peer_wq.shape[1])],
        out_shape=[jax.ShapeDtypeStruct((t, d), F32), jax.ShapeDtypeStruct((t, d), BF16),
                   jax.ShapeDtypeStruct((t, peer_wq.shape[1]), F32)],
        compiler_params=_params(("parallel",)),
        name="xattn_out_proj",
    )(x1, ox, wo_x, norm_ffn, peer_wq)


def _extract_max(s, order):
    m = jnp.max(s, axis=0, keepdims=True)
    big = jnp.float32(1e9)
    first = jnp.min(jnp.where(s == m, order, big), axis=0, keepdims=True)
    return m, first, order == first


def _cand_blocks():
    blocks = [(0, 0), (SUBLANES, 0)]
    blocks += [(0, b) for b in range(1, SUBLANES)]
    return blocks


def _route_body(part, n_parts, pq_ref, keys_ref, io_ref, jo_ref, go_ref, sc_ref, i_ref, j_ref, g_ref,
                *, heads, nkeys):
    tt = pq_ref.shape[0]
    k = P_TOPK
    neg = jnp.float32(-jnp.inf)
    order1 = lax.broadcasted_iota(I32, (nkeys, tt), 0).astype(F32)
    row_k = lax.broadcasted_iota(I32, (k, tt), 0).astype(F32)

    @pl.when(part == 0)
    def _():
        pq = pq_ref[...].astype(BF16)
        for h in range(heads):
            sc_ref[h] = _dot_nt(keys_ref[h], pq[:, h * LANES:(h + 1) * LANES])

    def head_body(h):
        sv, si = [], []
        for c in range(2):
            s = sc_ref[h, c * nkeys:(c + 1) * nkeys, :]
            vals, idxs = [], []
            for _ in range(k):
                m, first, hit = _extract_max(s, order1)
                vals.append(m)
                idxs.append(first)
                s = jnp.where(hit, neg, s)
                yield
            sv.append(jnp.concatenate(vals, axis=0))
            si.append(jnp.concatenate(idxs, axis=0))
        cand, flat = [], []
        row8 = lax.broadcasted_iota(I32, (SUBLANES, tt), 0).astype(F32)
        for a0, b in _cand_blocks():
            cand.append(sv[0][a0:a0 + SUBLANES] + sv[1][b:b + 1])
            flat.append((row8 + a0) * k + b)
        cand.append(sv[0][0:1] + sv[1][SUBLANES:k])
        flat.append(row8 + SUBLANES)
        cand, flat = (jnp.concatenate(v, axis=0) for v in (cand, flat))
        fs, fi, fj = [], [], []
        for _ in range(k):
            m, first, hit = _extract_max(cand, flat)
            fs.append(m)
            a = jnp.floor(first * (1.0 / k))
            b = first - a * k
            fi.append(jnp.sum(jnp.where(row_k == a, si[0], 0.0), axis=0, keepdims=True))
            fj.append(jnp.sum(jnp.where(row_k == b, si[1], 0.0), axis=0, keepdims=True))
            cand = jnp.where(hit, neg, cand)
            yield
        fs, fi, fj = (jnp.concatenate(v, axis=0) for v in (fs, fi, fj))
        e = jnp.exp(fs - fs[0:1])
        gate = e / jnp.sum(e, axis=0, keepdims=True)
        r0 = pl.multiple_of(h * k, k)
        i_ref[pl.ds(r0, k), :] = fi
        j_ref[pl.ds(r0, k), :] = fj
        g_ref[pl.ds(r0, k), :] = gate

    per = heads // n_parts
    for hh in range(per):
        yield from head_body(part * per + hh)

    @pl.when(part == n_parts - 1)
    def _():
        io_ref[...] = i_ref[...].T.astype(I32)
        jo_ref[...] = j_ref[...].T.astype(I32)
        go_ref[...] = g_ref[...].T


def _drain(*bodies):
    live = list(bodies)
    while live:
        for body in list(live):
            gen, per_round = body
            try:
                for _ in range(per_round):
                    next(gen)
            except StopIteration:
                live.remove(body)


def _route_kernel(*refs, heads, nkeys):
    _drain((_route_body(pl.program_id(1), ROUTE_PARTS, *refs, heads=heads, nkeys=nkeys), 1))


def _route_specs(pq, keys_pad, tt, tile_of):
    t = pq.shape[0]
    heads, two_nkeys, _ = keys_pad.shape
    hk = heads * P_TOPK
    in_specs = [pl.BlockSpec((tt, pq.shape[1]), lambda *g: (tile_of(*g), 0)),
                pl.BlockSpec(keys_pad.shape, lambda *g: (0, 0, 0))]
    out_specs = [pl.BlockSpec((tt, hk), lambda *g: (tile_of(*g), 0))] * 3
    out_shape = [jax.ShapeDtypeStruct((t, hk), I32), jax.ShapeDtypeStruct((t, hk), I32),
                 jax.ShapeDtypeStruct((t, hk), F32)]
    scratch = [pltpu.VMEM((heads, two_nkeys, tt), F32)] + [pltpu.VMEM((hk, tt), F32)] * 3
    return in_specs, out_specs, out_shape, scratch


def _route(pq, keys_pad, tt):
    heads, two_nkeys, _ = keys_pad.shape
    in_specs, out_specs, out_shape, scratch = _route_specs(pq, keys_pad, tt, lambda m, p: m)
    return pl.pallas_call(
        functools.partial(_route_kernel, heads=heads, nkeys=two_nkeys // 2),
        grid=(pq.shape[0] // tt, ROUTE_PARTS),
        in_specs=in_specs, out_specs=out_specs, out_shape=out_shape, scratch_shapes=scratch,
        compiler_params=_params(("parallel", "arbitrary")),
        name="peer_route",
    )(pq, keys_pad)


def _route_decode_kernel(pt_ref, *refs, heads, nkeys, a_dim, lam0, a_heads, n_route, n_decode, d_steps):
    del pt_ref
    s = pl.program_id(0)
    n_d_in = 8 + 2 * PAGES_PER_STEP
    d_in, r_in = refs[:n_d_in], refs[n_d_in:n_d_in + 2]
    outs = refs[n_d_in + 2:n_d_in + 6]
    d_scr, r_scr = refs[n_d_in + 6:n_d_in + 10], refs[n_d_in + 10:]

    def decode():
        return _decode_body(s % d_steps, d_steps, *d_in[:8], d_in[8:8 + PAGES_PER_STEP],
                            d_in[8 + PAGES_PER_STEP:], outs[0], *d_scr, a_dim=a_dim, lam0=lam0, heads=a_heads)

    def route():
        return _route_body(s % ROUTE_PARTS, ROUTE_PARTS, *r_in, *outs[1:], *r_scr, heads=heads, nkeys=nkeys)

    if n_route == n_decode:
        route_steps = (heads // ROUTE_PARTS) * 3 * P_TOPK
        _drain((decode(), 1), (route(), max(1, route_steps // (2 * PAGES_PER_STEP))))
    else:
        pl.when(s < n_decode)(lambda: _drain((decode(), 1)))
        pl.when(s < n_route)(lambda: _drain((route(), 1)))


def _route_and_decode(pq, keys_pad, tt, q, k_new, v_new, cache_k, cache_v, layer, page_table, subln, lams,
                      a_dim, lam0):
    bs, width = q.shape
    page, a_heads, hd = cache_k.shape[2:]
    n_pages = page_table.shape[1]
    d_steps = n_pages // PAGES_PER_STEP
    n_decode = bs * d_steps
    n_route = (pq.shape[0] // tt) * ROUTE_PARTS
    heads, two_nkeys, _ = keys_pad.shape
    pt = page_table.reshape(-1)
    q3, k3, v3 = (a.astype(F32).reshape(bs, a_heads, hd) for a in (q, k_new, v_new))
    sample = lambda s: jnp.minimum(s // d_steps, bs - 1)
    row = pl.BlockSpec((1, a_heads, hd), lambda s, pt: (sample(s), 0, 0))
    small = pl.BlockSpec((1, a_dim), lambda s, pt: (0, 0))

    def page_spec(i):
        def index(s, pt):
            sc = jnp.minimum(s, n_decode - 1)
            return (layer, pt[(sc // d_steps) * n_pages + (sc % d_steps) * PAGES_PER_STEP + i], 0, 0, 0)
        return pl.BlockSpec((1, 1, page, a_heads, hd), index)

    tile_of = lambda s, pt: jnp.minimum(s // ROUTE_PARTS, pq.shape[0] // tt - 1)
    r_in, r_out, r_shape, r_scr = _route_specs(pq, keys_pad, tt, tile_of)
    grid_spec = pltpu.PrefetchScalarGridSpec(
        num_scalar_prefetch=1,
        grid=(max(n_route, n_decode),),
        in_specs=[row, row, row, pl.BlockSpec((1, hd), lambda s, pt: (0, 0)), small, small, small, small]
        + [page_spec(i) for i in range(PAGES_PER_STEP)] * 2 + r_in,
        out_specs=[row] + r_out,
        scratch_shapes=[pltpu.VMEM((2 * a_heads, hd), BF16), pltpu.VMEM((2 * a_heads, 1), F32),
                        pltpu.VMEM((2 * a_heads, 1), F32), pltpu.VMEM((2 * a_heads, hd), F32)] + r_scr,
    )
    o, ii, jj, gate = pl.pallas_call(
        functools.partial(_route_decode_kernel, heads=heads, nkeys=two_nkeys // 2, a_dim=a_dim, lam0=lam0,
                          a_heads=a_heads, n_route=n_route, n_decode=n_decode, d_steps=d_steps),
        grid_spec=grid_spec,
        out_shape=[jax.ShapeDtypeStruct((bs, a_heads, hd), F32)] + r_shape,
        compiler_params=_params(("arbitrary",)),
        name="route_and_decode",
    )(pt, q3, k3, v3, subln, *lams, *([cache_k] * PAGES_PER_STEP), *([cache_v] * PAGES_PER_STEP),
      pq, keys_pad)
    return o.reshape(bs, width), ii, jj, gate


def _peer_act_kernel(hf_ref, u_ref, i_ref, j_ref, act_ref, *, nkeys):
    e = pl.program_id(1)

    @pl.when(e == 0)
    def _():
        act_ref[...] = jnp.zeros(act_ref.shape, F32)

    hf = hf_ref[...]
    ii = i_ref[...]
    jj = j_ref[...]
    act = act_ref[...]
    per = u_ref.shape[0] // nkeys
    for c0 in range(0, per, 2):
        s = _dot_nt(hf, u_ref[c0 * nkeys:(c0 + 2) * nkeys, :])
        for c in (c0, c0 + 1):
            picked = jnp.take_along_axis(s[:, (c - c0) * nkeys:(c - c0 + 1) * nkeys], jj, axis=1)
            act = jnp.where(ii == e * per + c, picked, act)
    act_ref[...] = act


def _peer_act(hf, u_bf16, ii, jj, nkeys, tt, eb):
    t, d = hf.shape
    hk = ii.shape[1]
    return pl.pallas_call(
        functools.partial(_peer_act_kernel, nkeys=nkeys),
        grid=(t // tt, u_bf16.shape[0] // eb),
        in_specs=[pl.BlockSpec((tt, d), lambda m, e: (m, 0)),
                  pl.BlockSpec((eb, d), lambda m, e: (e, 0)),
                  pl.BlockSpec((tt, hk), lambda m, e: (m, 0)),
                  pl.BlockSpec((tt, hk), lambda m, e: (m, 0))],
        out_specs=pl.BlockSpec((tt, hk), lambda m, e: (m, 0)),
        out_shape=jax.ShapeDtypeStruct((t, hk), F32),
        compiler_params=_params(("parallel", "arbitrary")),
        name="peer_act",
    )(hf, u_bf16, ii, jj)


def _peer_out_kernel(act_ref, g_ref, i_ref, j_ref, x_ref, v_ref, y_ref, wd_ref, w_ref, *, nkeys, pitch):
    e = pl.program_id(1)
    tt = act_ref.shape[0]
    hk = act_ref.shape[1]
    per = v_ref.shape[0] // nkeys

    @pl.when(e == 0)
    def _():
        a = act_ref[...]
        gelu = 0.5 * a * (1.0 + lax.erf(a * (2.0 ** -0.5)))
        w_ref[...] = g_ref[...] * gelu
        y_ref[...] = x_ref[...]
        sub = lax.broadcasted_iota(I32, (nkeys, hk), 0)

        def group(gidx, carry):
            t0 = pl.multiple_of(gidx * SUBLANES, SUBLANES)
            w8 = w_ref[pl.ds(t0, SUBLANES), :]
            i8 = i_ref[pl.ds(t0, SUBLANES), :]
            j8 = j_ref[pl.ds(t0, SUBLANES), :]
            for r in range(SUBLANES):
                wi = jnp.where(sub == i8[r:r + 1], w8[r:r + 1], 0.0).astype(BF16)
                oj = jnp.where(sub == j8[r:r + 1], 1.0, 0.0).astype(BF16)
                row0 = pl.multiple_of((t0 + r) * pitch, SUBLANES)
                wd_ref[pl.ds(row0, nkeys), :] = _dot_nt(wi, oj)
            return carry

        lax.fori_loop(0, tt // SUBLANES, group, 0)

    lhs = jnp.concatenate([wd_ref[pl.ds(e * per + c, tt, stride=pitch), :].astype(BF16)
                           for c in range(per)], axis=1)
    y_ref[...] += _dot(lhs, v_ref[...])


def _peer_out(act, gate, ii, jj, x2, v_bf16, nkeys, tt, eb):
    t, d = x2.shape
    hk = act.shape[1]
    pitch = nkeys + SUBLANES
    tok = lambda w: pl.BlockSpec((tt, w), lambda m, e: (m, 0))
    return pl.pallas_call(
        functools.partial(_peer_out_kernel, nkeys=nkeys, pitch=pitch),
        grid=(t // tt, v_bf16.shape[0] // eb),
        in_specs=[tok(hk), tok(hk), tok(hk), tok(hk), tok(d),
                  pl.BlockSpec((eb, d), lambda m, e: (e, 0))],
        out_specs=tok(d),
        out_shape=jax.ShapeDtypeStruct((t, d), F32),
        scratch_shapes=[pltpu.VMEM((tt * pitch, nkeys), F32), pltpu.VMEM((tt, hk), F32)],
        compiler_params=_params(("parallel", "arbitrary")),
        name="peer_out",
    )(act, gate, ii, jj, x2, v_bf16)


def _tile(n, pref):
    return pref if n % pref == 0 else n


def _layer_consts(lw, a_dim, heads_s, head_dim, inner, conv_dim):
    d = lw['w_in'].shape[0]
    a_width = lw['w_out'].shape[0] - inner
    tn = a_width
    cols = lw['w_in'].shape[1]
    assert cols - 4 * tn - conv_dim == heads_s <= LANES
    n_pad = 4 * tn + -(-(conv_dim + LANES) // tn) * tn
    w_pad = jnp.zeros((d, n_pad), BF16).at[:, :cols].set(lw['w_in'].astype(BF16))
    reps = a_width // a_dim
    colgain = jnp.concatenate([jnp.tile(lw['q_norm'], reps) * (a_dim ** -0.5),
                               jnp.tile(lw['k_norm'], reps)])[None, :]
    gid = jnp.arange(LANES) // a_dim
    bd = (gid[:, None] == gid[None, :]).astype(BF16)
    pad_h = lambda v: jnp.zeros((1, LANES), F32).at[0, :heads_s].set(v)
    tri = (jnp.arange(S_CHUNK)[None, :] <= jnp.arange(S_CHUNK)[:, None]).astype(BF16)
    expand = (jnp.arange(LANES)[:, None] == (jnp.arange(inner) // head_dim)[None, :]).astype(BF16)
    keys = lw['peer_keys']
    _, p_heads, nkeys, half = keys.shape
    keys_pad = jnp.zeros((p_heads, 2 * nkeys, 2 * half), BF16)
    keys_pad = keys_pad.at[:, :nkeys, :half].set(keys[0].astype(BF16))
    keys_pad = keys_pad.at[:, nkeys:, half:].set(keys[1].astype(BF16))
    return dict(
        w_pad=w_pad, colgain=colgain, bd=bd, tri=tri, expand=expand, keys_pad=keys_pad,
        norm_mix=lw['norm_mix'][None, :], subln=lw['subln'][None, :],
        lams=tuple(lw[n][None, :] for n in ('lambda_q1', 'lambda_k1', 'lambda_q2', 'lambda_k2')),
        conv_w=lw['conv_w'], conv_b=lw['conv_b'][None, :],
        dtb=pad_h(lw['dt_bias']), alog=pad_h(lw['a_log']),
        dskip=jnp.repeat(lw['d_skip'], head_dim)[None, :], ssm_norm=lw['ssm_norm'][None, :],
        w_out=lw['w_out'].astype(BF16), norm_x=lw['norm_x'][None, :], wq_x=lw['wq_x'].astype(BF16),
        qx_norm=lw['qx_norm'][None, :], kx_norm=lw['kx_norm'][None, :], norm_mem=lw['norm_mem'][None, :],
        w_kv=jnp.concatenate([lw['wk_x'], lw['wv_x']], axis=1).astype(BF16),
        wo_x=lw['wo_x'].astype(BF16), norm_ffn=lw['norm_ffn'][None, :],
        peer_wq=lw['peer_wq'].astype(BF16), peer_u=lw['peer_u'].astype(BF16),
        peer_v=lw['peer_v'].astype(BF16), nkeys=nkeys,
    )


def _peer(c, x2, hf, routing, tt_act, tt_out):
    t = x2.shape[0]
    nkeys = c['nkeys']
    ii, jj, gate = routing
    act = _peer_act(hf, c['peer_u'], ii, jj, nkeys, _tile(t, tt_act), PEER_ACT_BLOCKS * nkeys)
    return _peer_out(act, gate, ii, jj, x2, c['peer_v'], nkeys, _tile(t, tt_out), PEER_OUT_BLOCKS * nkeys)


def kernel(x_prompt, x_sample, cache_k, cache_v, cache_mem_k, cache_mem_v, state_conv, state_ssm, page_table, mem_prompt, norm_mix, w_in, q_norm, k_norm, lambda_q1, lambda_k1, lambda_q2, lambda_k2, subln, conv_w, conv_b, dt_bias, a_log, d_skip, ssm_norm, w_out, norm_x, norm_mem, wq_x, wk_x, wv_x, qx_norm, kx_norm, wo_x, norm_ffn, peer_wq, peer_keys, peer_u, peer_v):
    weights = dict(norm_mix=norm_mix, w_in=w_in, q_norm=q_norm, k_norm=k_norm, lambda_q1=lambda_q1,
                   lambda_k1=lambda_k1, lambda_q2=lambda_q2, lambda_k2=lambda_k2, subln=subln,
                   conv_w=conv_w, conv_b=conv_b, dt_bias=dt_bias, a_log=a_log, d_skip=d_skip,
                   ssm_norm=ssm_norm, w_out=w_out, norm_x=norm_x, norm_mem=norm_mem, wq_x=wq_x,
                   wk_x=wk_x, wv_x=wv_x, qx_norm=qx_norm, kx_norm=kx_norm, wo_x=wo_x,
                   norm_ffn=norm_ffn, peer_wq=peer_wq, peer_keys=peer_keys, peer_u=peer_u, peer_v=peer_v)
    depth = w_in.shape[0]
    bp, seq, d = x_prompt.shape
    bs, dec_seq, _ = x_sample.shape
    assert dec_seq == 1
    a_dim = q_norm.shape[-1]
    a_heads = cache_k.shape[3]
    a_width = a_heads * 2 * a_dim
    s_heads, head_dim, state = state_ssm.shape[2:]
    inner = s_heads * head_dim
    conv_dim = state_conv.shape[-1]
    n_mem, x_heads, x_dim = cache_mem_k.shape[2:]
    x_width = x_heads * x_dim
    assert a_width == inner and 2 * a_dim == LANES and 2 * head_dim == LANES and state == LANES

    yp = x_prompt.reshape(bp * seq, d)
    ys = x_sample.reshape(bs, d)
    outs = {n: [] for n in ('kp', 'vp', 'cp', 'sp', 'mkp', 'mvp', 'ks', 'vs', 'cs', 'ss')}
    for l in range(depth):
        lw = {n: w[l] for n, w in weights.items()}
        c = _layer_consts(lw, a_dim, s_heads, head_dim, inner, conv_dim)
        lam0 = 0.8 - 0.6 * math.exp(-0.3 * l)

        mk, mv = _memkv(mem_prompt.reshape(bp * n_mem, d), c['norm_mem'], c['w_kv'], c['kx_norm'],
                        _tile(bp * n_mem, 256))
        q, k, v, z, xd = _inproj(yp, c['norm_mix'], c['w_pad'], c['colgain'], c['bd'], a_dim,
                                 _tile(bp * seq, 512))
        o = _attn_prompt(q, k, v, c['subln'].T, c['lams'], bp, seq, a_dim, lam0, _tile(seq, 256))
        ym, st = _ssd_prompt(xd, z, c['conv_w'], c['conv_b'], c['dtb'], c['alog'], c['dskip'],
                             c['ssm_norm'], c['tri'], bp, seq, inner, state, conv_dim, head_dim)
        x1, qx = _outproj(yp, o, ym, c['w_out'], c['norm_x'], c['wq_x'], c['qx_norm'], _tile(bp * seq, 256))
        ox = _xattn_prompt(qx, mk, mv, bp, seq, n_mem, x_dim, _tile(seq, 256))
        x2, hf, pq = _xproj(x1, ox, c['wo_x'], c['norm_ffn'], c['peer_wq'], _tile(bp * seq, 256))
        qs, ks, vs, zs, xds = _inproj(ys, c['norm_mix'], c['w_pad'], c['colgain'], c['bd'], a_dim, bs)
        o_s, *routing = _route_and_decode(pq, c['keys_pad'], LANES, qs, ks, vs, cache_k, cache_v, l,
                                          page_table, c['subln'], c['lams'], a_dim, lam0)
        yp = _peer(c, x2, hf, routing, 512, 256)
        outs['kp'].append(k.reshape(bp, seq, a_heads, 2 * a_dim))
        outs['vp'].append(v.reshape(bp, seq, a_heads, 2 * a_dim))
        outs['cp'].append(xd.reshape(bp, seq, -1)[:, seq - (conv_w.shape[1] - 1):, :conv_dim])
        outs['sp'].append(st.reshape(bp, s_heads, head_dim, state))
        outs['mkp'].append(mk.reshape(bp, n_mem, x_heads, x_dim))
        outs['mvp'].append(mv.reshape(bp, n_mem, x_heads, x_dim))

        k, v, z, xd = ks, vs, zs, xds
        o = o_s.astype(BF16)
        conv_prev = state_conv[l]
        xbc, xdt_t, da = _ssd_step_prep(xd, jnp.swapaxes(conv_prev, 0, 1), c['conv_w'], c['conv_b'],
                                        c['dtb'], c['alog'], c['expand'], inner, conv_dim)
        hn, yt = _ssd_step(da[:, :s_heads].reshape(-1), state_ssm.reshape(depth * bs, inner, state), l,
                           xdt_t, xbc[:, inner:], state, head_dim)
        ym = _ssd_step_out(yt, xbc, z, c['dskip'], c['ssm_norm'], inner)
        x1, qx = _outproj(ys, o, ym, c['w_out'], c['norm_x'], c['wq_x'], c['qx_norm'], bs)
        ox = _xattn_decode(qx, cache_mem_k, cache_mem_v, l, 4)
        x2, hf, pq = _xproj(x1, ox, c['wo_x'], c['norm_ffn'], c['peer_wq'], bs)
        ys = _peer(c, x2, hf, _route(pq, c['keys_pad'], LANES), bs, bs)
        outs['ks'].append(k.reshape(bs, 1, a_heads, 2 * a_dim))
        outs['vs'].append(v.reshape(bs, 1, a_heads, 2 * a_dim))
        outs['cs'].append(jnp.concatenate([conv_prev[:, 1:], xd[:, None, :conv_dim]], axis=1))
        outs['ss'].append(hn.reshape(bs, s_heads, head_dim, state))

    st = lambda n: jnp.stack(outs[n]) if depth > 1 else outs[n][0][None]
    return (yp.reshape(bp, seq, d), ys.reshape(bs, 1, d), st('kp'), st('vp'), st('cp'), st('sp'),
            st('mkp'), st('mvp'), st('ks'), st('vs'), st('cs'), st('ss'))
```

```python
import functools
import math

import jax
import jax.numpy as jnp
import numpy as np
from jax import lax
from jax.experimental import pallas as pl
from jax.experimental.pallas import tpu as pltpu

F32 = jnp.float32
BF16 = jnp.bfloat16
I32 = jnp.int32
U32 = jnp.uint32
HI16 = np.uint32(0xFFFF0000)

EPS = 1e-6
LANES = 128
SUBLANES = 8
S_GROUPS = 2
S_CHUNK = 128
P_TOPK = 16
PAGES_PER_STEP = 8
PEER_ACT_BLOCKS = 8
PEER_OUT_BLOCKS = 8
ROUTE_PARTS = 4
VMEM_LIMIT = 56 << 20

NT_DIMS = (((1,), (1,)), ((), ()))


def _params(sem):
    return pltpu.CompilerParams(dimension_semantics=sem, vmem_limit_bytes=VMEM_LIMIT)


def _dot(a, b):
    return jnp.dot(a, b, preferred_element_type=F32)


def _dot_nt(a, b):
    return lax.dot_general(a, b, NT_DIMS, preferred_element_type=F32)


def _split3(x):
    hi = x.astype(BF16)
    r1 = x - hi.astype(F32)
    mid = r1.astype(BF16)
    lo = (r1 - mid.astype(F32)).astype(BF16)
    return hi, mid, lo


def _dot3(x, m_bf16):
    hi, mid, lo = _split3(x)
    return _dot(hi, m_bf16) + _dot(mid, m_bf16) + _dot(lo, m_bf16)


def _rms_rows(x, gain):
    ms = jnp.mean(x * x, axis=-1, keepdims=True)
    return x * lax.rsqrt(ms + EPS) * gain


def _head_rms(x, gain, width):
    outs = []
    for h in range(x.shape[1] // width):
        c = x[:, h * width:(h + 1) * width]
        outs.append(_rms_rows(c, gain))
    return outs[0] if len(outs) == 1 else jnp.concatenate(outs, axis=1)


def _silu(x):
    return x * (1.0 / (1.0 + jnp.exp(-x)))


def _softplus(x):
    return jnp.maximum(x, 0.0) + jnp.log1p(jnp.exp(-jnp.abs(x)))


def _inproj_kernel(x_ref, g_ref, w_ref, cg_ref, bd_ref, q_ref, k_ref, v_ref, z_ref, xd_ref,
                   xn_ref, *, group):
    n = pl.program_id(1)

    @pl.when(n == 0)
    def _():
        xn_ref[...] = _rms_rows(x_ref[...], g_ref[...]).astype(BF16)

    acc = _dot(xn_ref[...], w_ref[...])
    tn = acc.shape[1]

    def group_norm(dst_ref):
        for c in range(tn // LANES):
            a = acc[:, c * LANES:(c + 1) * LANES]
            sq = a * a
            hi = sq.astype(BF16)
            lo = (sq - hi.astype(F32)).astype(BF16)
            ss = _dot(hi, bd_ref[...]) + _dot(lo, bd_ref[...])
            y = a * lax.rsqrt(ss * (1.0 / group) + EPS) * cg_ref[:, c * LANES:(c + 1) * LANES]
            dst_ref[:, c * LANES:(c + 1) * LANES] = y.astype(dst_ref.dtype)

    @pl.when(n == 0)
    def _():
        group_norm(q_ref)

    @pl.when(n == 1)
    def _():
        group_norm(k_ref)

    @pl.when(n == 2)
    def _():
        v_ref[...] = acc

    @pl.when(n == 3)
    def _():
        z_ref[...] = acc

    @pl.when(n >= 4)
    def _():
        xd_ref[...] = acc


def _inproj(x, gain, w_pad, colgain, bd, a_dim, tm):
    t, d = x.shape
    n_pad = w_pad.shape[1]
    tn = colgain.shape[1] // 2
    nt = n_pad // tn
    nx = nt - 4
    grid = (t // tm, nt)
    row = lambda m, n: (m, 0)
    return pl.pallas_call(
        functools.partial(_inproj_kernel, group=a_dim),
        grid=grid,
        in_specs=[
            pl.BlockSpec((tm, d), row),
            pl.BlockSpec((1, d), lambda m, n: (0, 0)),
            pl.BlockSpec((d, tn), lambda m, n: (0, n)),
            pl.BlockSpec((1, tn), lambda m, n: (0, jnp.minimum(n, 1))),
            pl.BlockSpec((LANES, LANES), lambda m, n: (0, 0)),
        ],
        out_specs=[
            pl.BlockSpec((tm, tn), row),
            pl.BlockSpec((tm, tn), row),
            pl.BlockSpec((tm, tn), row),
            pl.BlockSpec((tm, tn), row),
            pl.BlockSpec((tm, tn), lambda m, n: (m, jnp.clip(n - 4, 0, nx - 1))),
        ],
        out_shape=[
            jax.ShapeDtypeStruct((t, tn), BF16),
            jax.ShapeDtypeStruct((t, tn), F32),
            jax.ShapeDtypeStruct((t, tn), F32),
            jax.ShapeDtypeStruct((t, tn), F32),
            jax.ShapeDtypeStruct((t, nx * tn), F32),
        ],
        scratch_shapes=[pltpu.VMEM((tm, d), BF16)],
        compiler_params=_params(("parallel", "arbitrary")),
        name="in_proj",
    )(x, gain, w_pad, colgain, bd)


def _lambda(lq1, lk1, lq2, lk2, lam0):
    return (jnp.exp(jnp.sum(lq1[...] * lk1[...], axis=-1, keepdims=True))
            - jnp.exp(jnp.sum(lq2[...] * lk2[...], axis=-1, keepdims=True)) + lam0)


def _attn_prompt_kernel(q_ref, k_ref, v_ref, sub_ref, lq1, lk1, lq2, lk2, o_ref,
                        kb_ref, vt_ref, m_ref, l_ref, acc_ref, *, a_dim, lam0):
    qi = pl.program_id(2)
    tq = q_ref.shape[0]
    n_kv = vt_ref.shape[0]

    @pl.when(qi == 0)
    def _():
        kb_ref[...] = k_ref[...].astype(BF16)
        for j in range(n_kv):
            vt_ref[j] = v_ref[j * tq:(j + 1) * tq, :].T.astype(BF16)

    q = q_ref[...]
    lane = lax.broadcasted_iota(I32, q.shape, 1)
    zero = jnp.zeros_like(q)
    q2 = jnp.concatenate([jnp.where(lane < a_dim, q, zero), jnp.where(lane >= a_dim, q, zero)], axis=0)

    m_ref[...] = jnp.full(m_ref.shape, -jnp.inf, F32)
    l_ref[...] = jnp.zeros(l_ref.shape, F32)
    acc_ref[...] = jnp.zeros(acc_ref.shape, F32)

    def step(j, masked, tiles):
        off = pl.multiple_of(j * tq, tq)
        s = _dot_nt(kb_ref[pl.ds(off, tiles * tq), :], q2)
        if masked:
            r = lax.broadcasted_iota(I32, s.shape, 0)
            c = lax.broadcasted_iota(I32, s.shape, 1)
            c = jnp.where(c >= tq, c - tq, c)
            s = jnp.where(r <= c, s, -jnp.inf)
        m_old = m_ref[...]
        m_new = jnp.maximum(m_old, jnp.max(s, axis=0, keepdims=True))
        alpha = jnp.exp(m_old - m_new)
        p = jnp.exp(s - m_new)
        l_ref[...] = alpha * l_ref[...] + jnp.sum(p, axis=0, keepdims=True)
        pb = p.astype(BF16)
        pv = _dot(vt_ref[j], pb[0:tq])
        for i in range(1, tiles):
            pv = pv + _dot(vt_ref[j + i], pb[i * tq:(i + 1) * tq])
        acc_ref[...] = alpha * acc_ref[...] + pv
        m_ref[...] = m_new

    def body(jp, carry):
        step(2 * jp, False, 2)
        return carry

    lax.fori_loop(0, qi // 2, body, 0)

    @pl.when(qi % 2 == 1)
    def _():
        step(qi - 1, False, 1)

    step(qi, True, 1)

    lam = _lambda(lq1, lk1, lq2, lk2, lam0)
    o1 = acc_ref[:, 0:tq] / l_ref[:, 0:tq]
    o2 = acc_ref[:, tq:2 * tq] / l_ref[:, tq:2 * tq]
    o = o1 - lam * o2
    ms = jnp.mean(o * o, axis=0, keepdims=True)
    o = o * lax.rsqrt(ms + EPS) * sub_ref[...] * (1.0 - lam0)
    o_ref[...] = o.T.astype(o_ref.dtype)


def _attn_prompt(q, k, v, subln_col, lams, batch, seq, a_dim, lam0, tq):
    t, width = q.shape
    hd = 2 * a_dim
    heads = width // hd
    nq = seq // tq
    small = pl.BlockSpec((1, a_dim), lambda b, h, i: (0, 0))
    return pl.pallas_call(
        functools.partial(_attn_prompt_kernel, a_dim=a_dim, lam0=lam0),
        grid=(batch, heads, nq),
        in_specs=[
            pl.BlockSpec((tq, hd), lambda b, h, i: (b * nq + i, h)),
            pl.BlockSpec((seq, hd), lambda b, h, i: (b, h)),
            pl.BlockSpec((seq, hd), lambda b, h, i: (b, h)),
            pl.BlockSpec((hd, 1), lambda b, h, i: (0, 0)),
            small, small, small, small,
        ],
        out_specs=pl.BlockSpec((tq, hd), lambda b, h, i: (b * nq + i, h)),
        out_shape=jax.ShapeDtypeStruct((t, width), BF16),
        scratch_shapes=[pltpu.VMEM((seq, hd), BF16), pltpu.VMEM((nq, hd, tq), BF16),
                        pltpu.VMEM((1, 2 * tq), F32), pltpu.VMEM((1, 2 * tq), F32),
                        pltpu.VMEM((hd, 2 * tq), F32)],
        compiler_params=_params(("parallel", "parallel", "arbitrary")),
        name="diff_attn_prompt",
    )(q, k, v, subln_col, *lams)


def _decode_body(p_id, n_p, q_ref, kn_ref, vn_ref, sub_ref, lq1, lk1, lq2, lk2, kp, vp,
                 o_ref, qr_ref, m_ref, l_ref, acc_ref, *, a_dim, lam0, heads):
    rows = 2 * heads
    hd = 2 * a_dim
    page = kp[0].shape[2]
    cols = page * heads

    @pl.when(p_id == 0)
    def _():
        q8 = q_ref[0]
        lane = lax.broadcasted_iota(I32, q8.shape, 1)
        qr_ref[...] = jnp.concatenate([jnp.where(lane < a_dim, q8, 0.0),
                                       jnp.where(lane >= a_dim, q8, 0.0)], axis=0).astype(BF16)
        m_ref[...] = jnp.full(m_ref.shape, -jnp.inf, F32)
        l_ref[...] = jnp.zeros(l_ref.shape, F32)
        acc_ref[...] = jnp.zeros(acc_ref.shape, F32)

    qr = qr_ref[...]
    r = lax.broadcasted_iota(I32, (rows, cols), 0)
    c = lax.broadcasted_iota(I32, (rows, cols), 1)
    same_head = (r % heads) == (c % heads)

    s = []
    for i in range(PAGES_PER_STEP):
        s.append(jnp.where(same_head, _dot_nt(qr, kp[i][0, 0].reshape(cols, hd).astype(BF16)), -jnp.inf))
        yield
    m_old = m_ref[...]
    m_new = m_old
    for si in s:
        m_new = jnp.maximum(m_new, jnp.max(si, axis=-1, keepdims=True))
    alpha = jnp.exp(m_old - m_new)
    l_new = alpha * l_ref[...]
    acc = alpha * acc_ref[...]
    for i, si in enumerate(s):
        p = jnp.exp(si - m_new)
        l_new = l_new + jnp.sum(p, axis=-1, keepdims=True)
        acc = acc + _dot(p.astype(BF16), vp[i][0, 0].reshape(cols, hd).astype(BF16))
        yield
    m_ref[...] = m_new
    l_ref[...] = l_new
    acc_ref[...] = acc

    @pl.when(p_id == n_p - 1)
    def _():
        kn = kn_ref[0].astype(BF16).astype(F32)
        vn = vn_ref[0].astype(BF16).astype(F32)
        kn2 = jnp.concatenate([kn, kn], axis=0)
        vn2 = jnp.concatenate([vn, vn], axis=0)
        s_new = jnp.sum(qr.astype(F32) * kn2, axis=-1, keepdims=True)
        m_fin = jnp.maximum(m_new, s_new)
        a_fin = jnp.exp(m_new - m_fin)
        p_new = jnp.exp(s_new - m_fin)
        l_fin = a_fin * l_new + p_new
        acc_fin = a_fin * acc + p_new.astype(BF16).astype(F32) * vn2

        lam = _lambda(lq1, lk1, lq2, lk2, lam0)
        o1 = acc_fin[0:heads, :] / l_fin[0:heads, :]
        o2 = acc_fin[heads:rows, :] / l_fin[heads:rows, :]
        o_ref[0] = _rms_rows(o1 - lam * o2, sub_ref[...]) * (1.0 - lam0)


def _conv_taps(cur, tail, w_ref, b_ref):
    n_tap = w_ref.shape[0]
    out = b_ref[...] + w_ref[n_tap - 1:n_tap, :] * cur
    row8 = lax.broadcasted_iota(I32, (SUBLANES, cur.shape[1]), 0)
    for s in range(1, n_tap):
        rolled = pltpu.roll(cur, s, 0)
        head = jnp.where(row8 < s, pltpu.roll(tail, s, 0), rolled[0:SUBLANES])
        shifted = jnp.concatenate([head, rolled[SUBLANES:]], axis=0)
        out = out + w_ref[n_tap - 1 - s:n_tap - s, :] * shifted
    return out


def _ssd_prompt_kernel(xd_ref, z_ref, cw_ref, cb_ref, dtb_ref, alog_ref, dsk_ref, ng_ref, tri_ref,
                       y_ref, st_ref, tail_ref, h_ref, yg_ref, *, inner, state, conv_dim, head_dim):
    c_id = pl.program_id(1)
    ch = xd_ref.shape[0]
    pairs = h_ref.shape[0]
    rep = (inner // head_dim) // S_GROUPS

    @pl.when(c_id == 0)
    def _():
        tail_ref[...] = jnp.zeros(tail_ref.shape, F32)
        h_ref[...] = jnp.zeros(h_ref.shape, F32)

    pre = xd_ref[:, 0:conv_dim]
    xbc = _silu(_conv_taps(pre, tail_ref[...], cw_ref, cb_ref))
    tail_ref[...] = pre[ch - SUBLANES:ch, :]

    dtv = _softplus(xd_ref[:, conv_dim:conv_dim + LANES] + dtb_ref[...])
    a = dtv * (-jnp.exp(alog_ref[...]))
    a_hi, a_mid, a_lo = _split3(a)
    tri = tri_ref[...]
    a_cum = _dot(tri, a_hi) + _dot(tri, a_mid) + _dot(tri, a_lo)
    a_cum_t = a_cum.T
    row = lax.broadcasted_iota(I32, (ch, ch), 0)
    col = lax.broadcasted_iota(I32, (ch, ch), 1)
    lane = lax.broadcasted_iota(I32, (ch, LANES), 1)
    first = lane < head_dim
    causal = col <= row

    ssq = jnp.zeros((ch, 1), F32)
    bms = [xbc[:, inner + g * state:inner + (g + 1) * state].astype(BF16) for g in range(S_GROUPS)]
    cms = [xbc[:, inner + (S_GROUPS + g) * state:inner + (S_GROUPS + g + 1) * state].astype(BF16)
           for g in range(S_GROUPS)]
    cb = [_dot_nt(cms[g], bms[g]) for g in range(S_GROUPS)]
    for pr in range(pairs):
        g = (2 * pr) // rep
        bm, cm = bms[g], cms[g]
        xs = xbc[:, pr * LANES:(pr + 1) * LANES]
        ha, hb = 2 * pr, 2 * pr + 1
        dt2 = jnp.where(first, dtv[:, ha:ha + 1], dtv[:, hb:hb + 1])
        xdt = xs * dt2
        ac2 = jnp.where(first, a_cum[:, ha:ha + 1], a_cum[:, hb:hb + 1])
        last2 = ac2[ch - 1:ch, :]
        y = jnp.zeros((ch, LANES), F32)
        for hh, keep in ((ha, first), (hb, jnp.logical_not(first))):
            seg = a_cum[:, hh:hh + 1] - a_cum_t[hh:hh + 1, :]
            lm = jnp.exp(jnp.where(causal, seg, -jnp.inf))
            y = y + _dot((cb[g] * lm).astype(BF16), jnp.where(keep, xdt, 0.0).astype(BF16))
        hprev = h_ref[pr]
        y = y + _dot_nt(cm, hprev.astype(BF16)) * jnp.exp(ac2)
        xdec = (xdt * jnp.exp(last2 - ac2)).T
        upd = _dot(xdec.astype(BF16), bm)
        rowp = lax.broadcasted_iota(I32, hprev.shape, 0)
        e_last = jnp.exp(last2)
        decay = jnp.where(rowp < head_dim, e_last[:, 0:1], e_last[:, head_dim:head_dim + 1])
        h_ref[pr] = decay * hprev + upd
        y = y + dsk_ref[:, pr * LANES:(pr + 1) * LANES] * xs
        y = y * _silu(z_ref[:, pr * LANES:(pr + 1) * LANES])
        yg_ref[:, pr * LANES:(pr + 1) * LANES] = y
        ssq = ssq + jnp.sum(y * y, axis=-1, keepdims=True)

    y_ref[...] = (yg_ref[...] * lax.rsqrt(ssq * (1.0 / inner) + EPS) * ng_ref[...]).astype(y_ref.dtype)

    @pl.when(c_id == pl.num_programs(1) - 1)
    def _():
        st_ref[0] = h_ref[...]


def _ssd_prompt(xd, z, conv_w, conv_b, dtb_pad, alog_pad, dskip_exp, norm_g, tri, batch, seq,
                inner, state, conv_dim, head_dim):
    t = xd.shape[0]
    nc = seq // S_CHUNK
    pairs = inner // LANES
    const = lambda shape: pl.BlockSpec(shape, lambda b, c: (0,) * len(shape))
    return pl.pallas_call(
        functools.partial(_ssd_prompt_kernel, inner=inner, state=state, conv_dim=conv_dim,
                          head_dim=head_dim),
        grid=(batch, nc),
        in_specs=[
            pl.BlockSpec((S_CHUNK, xd.shape[1]), lambda b, c: (b * nc + c, 0)),
            pl.BlockSpec((S_CHUNK, inner), lambda b, c: (b * nc + c, 0)),
            const(conv_w.shape), const(conv_b.shape), const(dtb_pad.shape), const(alog_pad.shape),
            const(dskip_exp.shape), const(norm_g.shape), const(tri.shape),
        ],
        out_specs=[
            pl.BlockSpec((S_CHUNK, inner), lambda b, c: (b * nc + c, 0)),
            pl.BlockSpec((1, pairs, LANES, state), lambda b, c: (b, 0, 0, 0)),
        ],
        out_shape=[jax.ShapeDtypeStruct((t, inner), BF16),
                   jax.ShapeDtypeStruct((batch, pairs, LANES, state), F32)],
        scratch_shapes=[pltpu.VMEM((SUBLANES, conv_dim), F32), pltpu.VMEM((pairs, LANES, state), F32),
                        pltpu.VMEM((S_CHUNK, inner), F32)],
        compiler_params=_params(("parallel", "arbitrary")),
        name="ssd_prompt",
    )(xd, z, conv_w, conv_b, dtb_pad, alog_pad, dskip_exp, norm_g, tri)


def _ssd_step_prep_kernel(xd_ref, cs_ref, cw_ref, cb_ref, dtb_ref, alog_ref, exp_ref,
                          xbc_ref, xdt_t_ref, da_ref, *, inner, conv_dim):
    n_tap = cw_ref.shape[0]
    out = cb_ref[...] + cw_ref[n_tap - 1:n_tap, :] * xd_ref[:, 0:conv_dim]
    for j in range(n_tap - 1):
        out = out + cw_ref[j:j + 1, :] * cs_ref[j]
    xbc = _silu(out)
    xbc_ref[...] = xbc
    dtv = _softplus(xd_ref[:, conv_dim:conv_dim + LANES] + dtb_ref[...])
    da_ref[...] = jnp.exp(dtv * (-jnp.exp(alog_ref[...])))
    dt_exp = _dot3(dtv, exp_ref[...])
    xdt_t_ref[...] = (xbc[:, 0:inner] * dt_exp).T


def _ssd_step_prep(xd, conv_state_t, conv_w, conv_b, dtb_pad, alog_pad, expand, inner, conv_dim):
    bs = xd.shape[0]
    return pl.pallas_call(
        functools.partial(_ssd_step_prep_kernel, inner=inner, conv_dim=conv_dim),
        out_shape=[jax.ShapeDtypeStruct((bs, conv_dim), F32),
                   jax.ShapeDtypeStruct((inner, bs), F32),
                   jax.ShapeDtypeStruct((bs, LANES), F32)],
        compiler_params=pltpu.CompilerParams(vmem_limit_bytes=VMEM_LIMIT),
        name="ssd_step_prep",
    )(xd, conv_state_t, conv_w, conv_b, dtb_pad, alog_pad, expand)


def _ssd_step_kernel(da_ref, h_ref, xdt_t_ref, bc_ref, hn_ref, yt_ref, *, state, head_dim):
    b = pl.program_id(0)
    bs = bc_ref.shape[0]
    rows = h_ref.shape[1]
    gr = rows // S_GROUPS

    @pl.when(b == 0)
    def _():
        yt_ref[...] = jnp.zeros(yt_ref.shape, F32)

    sel = lax.broadcasted_iota(I32, (bs, state), 0) == b
    for g in range(S_GROUPS):
        b_sel = jnp.where(sel, bc_ref[:, g * state:(g + 1) * state], 0.0)
        c_sel = jnp.where(sel, bc_ref[:, (S_GROUPS + g) * state:(S_GROUPS + g + 1) * state], 0.0)
        xt = xdt_t_ref[g * gr:(g + 1) * gr, :]
        x_hi = xt.astype(BF16)
        x_lo = (xt - x_hi.astype(F32)).astype(BF16)
        b_hi = b_sel.astype(BF16)
        b_lo = (b_sel - b_hi.astype(F32)).astype(BF16)
        upd = _dot(x_hi, b_hi) + _dot(x_lo, b_hi) + _dot(x_hi, b_lo)
        hn_g = []
        for hh in range(gr // head_dim):
            r0 = g * gr + hh * head_dim
            da = da_ref[b * (rows // head_dim) + r0 // head_dim]
            hn = da * h_ref[0, r0:r0 + head_dim, :] + upd[hh * head_dim:(hh + 1) * head_dim, :]
            hn_ref[0, r0:r0 + head_dim, :] = hn
            hn_g.append(hn.astype(BF16))
        hn_g = jnp.concatenate(hn_g, axis=0)
        yt_ref[g * gr:(g + 1) * gr, :] += _dot_nt(hn_g, c_sel.astype(BF16))


def _ssd_step(da, h_all, layer, xdt_t, bc, state, head_dim):
    rows = h_all.shape[1]
    bs = bc.shape[0]
    h = jax.ShapeDtypeStruct((bs, rows, state), F32)
    return pl.pallas_call(
        functools.partial(_ssd_step_kernel, state=state, head_dim=head_dim),
        grid=(bs,),
        in_specs=[
            pl.BlockSpec(memory_space=pltpu.SMEM),
            pl.BlockSpec((1, rows, state), lambda b: (layer * bs + b, 0, 0)),
            pl.BlockSpec(xdt_t.shape, lambda b: (0, 0)),
            pl.BlockSpec(bc.shape, lambda b: (0, 0)),
        ],
        out_specs=[
            pl.BlockSpec((1, rows, state), lambda b: (b, 0, 0)),
            pl.BlockSpec((rows, bs), lambda b: (0, 0)),
        ],
        out_shape=[jax.ShapeDtypeStruct(h.shape, F32), jax.ShapeDtypeStruct((rows, bs), F32)],
        compiler_params=_params(("arbitrary",)),
        name="ssd_step",
    )(da, h_all, xdt_t, bc)


def _ssd_step_out_kernel(yt_ref, xbc_ref, z_ref, dsk_ref, ng_ref, y_ref, *, inner):
    y = yt_ref[...].T + dsk_ref[...] * xbc_ref[:, 0:inner]
    y = y * _silu(z_ref[...])
    y_ref[...] = _rms_rows(y, ng_ref[...]).astype(y_ref.dtype)


def _ssd_step_out(yt, xbc, z, dskip_exp, norm_g, inner):
    bs = xbc.shape[0]
    return pl.pallas_call(
        functools.partial(_ssd_step_out_kernel, inner=inner),
        out_shape=jax.ShapeDtypeStruct((bs, inner), BF16),
        compiler_params=pltpu.CompilerParams(vmem_limit_bytes=VMEM_LIMIT),
        name="ssd_step_out",
    )(yt, xbc, z, dskip_exp, norm_g)


def _outproj_kernel(x_ref, o_ref, y_ref, wo_ref, nx_ref, wq_ref, qn_ref, x1_ref, qx_ref, *, a_width, x_dim):
    x1 = x_ref[...] + _dot(o_ref[...], wo_ref[0:a_width, :]) + _dot(y_ref[...], wo_ref[a_width:, :])
    x1_ref[...] = x1
    hx = _rms_rows(x1, nx_ref[...]).astype(BF16)
    qx = _dot(hx, wq_ref[...])
    qx_ref[...] = _head_rms(qx, qn_ref[...], x_dim).astype(qx_ref.dtype)


def _outproj(x, o, y, w_out, norm_x, wq_x, qx_norm, tm):
    t, d = x.shape
    a_width = o.shape[1]
    x_width = wq_x.shape[1]
    row = lambda w: pl.BlockSpec((tm, w), lambda m: (m, 0))
    const = lambda a: pl.BlockSpec(a.shape, lambda m: (0, 0))
    return pl.pallas_call(
        functools.partial(_outproj_kernel, a_width=a_width, x_dim=qx_norm.shape[1]),
        grid=(t // tm,),
        in_specs=[row(d), row(a_width), row(y.shape[1]), const(w_out), const(norm_x), const(wq_x),
                  const(qx_norm)],
        out_specs=[row(d), row(x_width)],
        out_shape=[jax.ShapeDtypeStruct((t, d), F32), jax.ShapeDtypeStruct((t, x_width), BF16)],
        compiler_params=_params(("parallel",)),
        name="out_proj",
    )(x, o, y, w_out, norm_x, wq_x, qx_norm)


def _memkv_kernel(mem_ref, g_ref, w_ref, kn_ref, mk_ref, mv_ref, *, x_dim):
    m = _rms_rows(mem_ref[...], g_ref[...]).astype(BF16)
    kv = _dot(m, w_ref[...])
    xw = mk_ref.shape[1]
    mk_ref[...] = _head_rms(kv[:, 0:xw], kn_ref[...], x_dim)
    mv_ref[...] = kv[:, xw:]


def _memkv(mem, norm_mem, w_kv, kx_norm, tm):
    t, d = mem.shape
    xw = w_kv.shape[1] // 2
    const = lambda a: pl.BlockSpec(a.shape, lambda m: (0, 0))
    return pl.pallas_call(
        functools.partial(_memkv_kernel, x_dim=kx_norm.shape[1]),
        grid=(t // tm,),
        in_specs=[pl.BlockSpec((tm, d), lambda m: (m, 0)), const(norm_mem), const(w_kv), const(kx_norm)],
        out_specs=[pl.BlockSpec((tm, xw), lambda m: (m, 0))] * 2,
        out_shape=[jax.ShapeDtypeStruct((t, xw), F32)] * 2,
        compiler_params=_params(("parallel",)),
        name="memory_kv",
    )(mem, norm_mem, w_kv, kx_norm)


def _xattn_heads(q, mk, mv, x_dim):
    scale = x_dim ** -0.5
    outs = []
    for h in range(q.shape[1] // x_dim):
        sl = slice(h * x_dim, (h + 1) * x_dim)
        s = _dot_nt(q[:, sl], mk[:, sl].astype(BF16)) * scale
        s = s - jnp.max(s, axis=-1, keepdims=True)
        e = jnp.exp(s)
        p = e / jnp.sum(e, axis=-1, keepdims=True)
        outs.append(_dot(p.astype(BF16), mv[:, sl].astype(BF16)))
    return jnp.concatenate(outs, axis=1)


def _xattn_prompt_kernel(q_ref, mk_ref, mv_ref, o_ref, *, x_dim):
    o_ref[...] = _xattn_heads(q_ref[...], mk_ref[...], mv_ref[...], x_dim).astype(o_ref.dtype)


def _xattn_prompt(qx, mk, mv, batch, seq, n_mem, x_dim, tq):
    t, xw = qx.shape
    nq = seq // tq
    return pl.pallas_call(
        functools.partial(_xattn_prompt_kernel, x_dim=x_dim),
        grid=(batch, nq),
        in_specs=[pl.BlockSpec((tq, xw), lambda b, i: (b * nq + i, 0)),
                  pl.BlockSpec((n_mem, xw), lambda b, i: (b, 0)),
                  pl.BlockSpec((n_mem, xw), lambda b, i: (b, 0))],
        out_specs=pl.BlockSpec((tq, xw), lambda b, i: (b * nq + i, 0)),
        out_shape=jax.ShapeDtypeStruct((t, xw), BF16),
        compiler_params=_params(("parallel", "parallel")),
        name="xattn_prompt",
    )(qx, mk, mv)


def _xattn_decode_kernel(q_ref, mk_ref, mv_ref, o_ref, *, x_dim):
    scale = x_dim ** -0.5
    for i in range(q_ref.shape[0]):
        s = jnp.sum(mk_ref[0, i] * q_ref[i][None], axis=-1, keepdims=True) * scale
        e = jnp.exp(s - jnp.max(s, axis=0, keepdims=True))
        p = e / jnp.sum(e, axis=0, keepdims=True)
        o_ref[i] = jnp.sum(p * mv_ref[0, i], axis=0)


def _xattn_decode(qx, mem_k, mem_v, layer, per_step):
    bs, xw = qx.shape
    n_mem, heads, x_dim = mem_k.shape[2:]
    q3 = qx.astype(F32).reshape(bs, heads, x_dim)
    mem_spec = pl.BlockSpec((1, per_step, n_mem, heads, x_dim), lambda b: (layer, b, 0, 0, 0))
    out = pl.pallas_call(
        functools.partial(_xattn_decode_kernel, x_dim=x_dim),
        grid=(bs // per_step,),
        in_specs=[pl.BlockSpec((per_step, heads, x_dim), lambda b: (b, 0, 0)), mem_spec, mem_spec],
        out_specs=pl.BlockSpec((per_step, heads, x_dim), lambda b: (b, 0, 0)),
        out_shape=jax.ShapeDtypeStruct((bs, heads, x_dim), F32),
        compiler_params=_params(("parallel",)),
        name="xattn_decode",
    )(q3, mem_k, mem_v)
    return out.reshape(bs, xw).astype(BF16)


def _xproj_kernel(x1_ref, ox_ref, wo_ref, nf_ref, wq_ref, x2_ref, hf_ref, pq_ref):
    x2 = x1_ref[...] + _dot(ox_ref[...], wo_ref[...])
    x2_ref[...] = x2
    hf = _rms_rows(x2, nf_ref[...]).astype(BF16)
    hf_ref[...] = hf
    pq_ref[...] = _dot(hf, wq_ref[...])


def _xproj(x1, ox, wo_x, norm_ffn, peer_wq, tm):
    t, d = x1.shape
    row = lambda w: pl.BlockSpec((tm, w), lambda m: (m, 0))
    const = lambda a: pl.BlockSpec(a.shape, lambda m: (0, 0))
    return pl.pallas_call(
        _xproj_kernel,
        grid=(t // tm,),
        in_specs=[row(d), row(ox.shape[1]), const(wo_x), const(norm_ffn), const(peer_wq)],
        out_specs=[row(d), row(d), row(peer_wq.shape[1])],
        out_shape=[jax.ShapeDtypeStruct((t, d), F32), jax.ShapeDtypeStruct((t, d), BF16),
                   jax.ShapeDtypeStruct((t, peer_wq.shape[1]), F32)],
        compiler_params=_params(("parallel",)),
        name="xattn_out_proj",
    )(x1, ox, wo_x, norm_ffn, peer_wq)


def _extract_max(s, order):
    m = jnp.max(s, axis=0, keepdims=True)
    big = jnp.float32(1e9)
    first = jnp.min(jnp.where(s == m, order, big), axis=0, keepdims=True)
    return m, first, order == first


def _cand_blocks():
    blocks = [(0, 0), (SUBLANES, 0)]
    blocks += [(0, b) for b in range(1, SUBLANES)]
    return blocks


def _route_body(part, n_parts, pq_ref, keys_ref, io_ref, jo_ref, go_ref, sc_ref, i_ref, j_ref, g_ref,
                *, heads, nkeys):
    tt = pq_ref.shape[0]
    k = P_TOPK
    neg = jnp.float32(-jnp.inf)
    order1 = lax.broadcasted_iota(I32, (nkeys, tt), 0).astype(F32)
    row_k = lax.broadcasted_iota(I32, (k, tt), 0).astype(F32)

    @pl.when(part == 0)
    def _():
        pq = pq_ref[...].astype(BF16)
        for h in range(heads):
            sc_ref[h] = _dot_nt(keys_ref[h], pq[:, h * LANES:(h + 1) * LANES])

    def head_body(h):
        sv, si = [], []
        for c in range(2):
            s = sc_ref[h, c * nkeys:(c + 1) * nkeys, :]
            vals, idxs = [], []
            for _ in range(k):
                m, first, hit = _extract_max(s, order1)
                vals.append(m)
                idxs.append(first)
                s = jnp.where(hit, neg, s)
                yield
            sv.append(jnp.concatenate(vals, axis=0))
            si.append(jnp.concatenate(idxs, axis=0))
        cand, flat = [], []
        row8 = lax.broadcasted_iota(I32, (SUBLANES, tt), 0).astype(F32)
        for a0, b in _cand_blocks():
            cand.append(sv[0][a0:a0 + SUBLANES] + sv[1][b:b + 1])
            flat.append((row8 + a0) * k + b)
        cand.append(sv[0][0:1] + sv[1][SUBLANES:k])
        flat.append(row8 + SUBLANES)
        cand, flat = (jnp.concatenate(v, axis=0) for v in (cand, flat))
        fs, fi, fj = [], [], []
        for _ in range(k):
            m, first, hit = _extract_max(cand, flat)
            fs.append(m)
            a = jnp.floor(first * (1.0 / k))
            b = first - a * k
            fi.append(jnp.sum(jnp.where(row_k == a, si[0], 0.0), axis=0, keepdims=True))
            fj.append(jnp.sum(jnp.where(row_k == b, si[1], 0.0), axis=0, keepdims=True))
            cand = jnp.where(hit, neg, cand)
            yield
        fs, fi, fj = (jnp.concatenate(v, axis=0) for v in (fs, fi, fj))
        e = jnp.exp(fs - fs[0:1])
        gate = e / jnp.sum(e, axis=0, keepdims=True)
        r0 = pl.multiple_of(h * k, k)
        i_ref[pl.ds(r0, k), :] = fi
        j_ref[pl.ds(r0, k), :] = fj
        g_ref[pl.ds(r0, k), :] = gate

    per = heads // n_parts
    for hh in range(per):
        yield from head_body(part * per + hh)

    @pl.when(part == n_parts - 1)
    def _():
        io_ref[...] = i_ref[...].T.astype(I32)
        jo_ref[...] = j_ref[...].T.astype(I32)
        go_ref[...] = g_ref[...].T


def _drain(*bodies):
    live = list(bodies)
    while live:
        for body in list(live):
            gen, per_round = body
            try:
                for _ in range(per_round):
                    next(gen)
            except StopIteration:
                live.remove(body)


def _route_kernel(*refs, heads, nkeys):
    _drain((_route_body(pl.program_id(1), ROUTE_PARTS, *refs, heads=heads, nkeys=nkeys), 1))


def _route_specs(pq, keys_pad, tt, tile_of):
    t = pq.shape[0]
    heads, two_nkeys, _ = keys_pad.shape
    hk = heads * P_TOPK
    in_specs = [pl.BlockSpec((tt, pq.shape[1]), lambda *g: (tile_of(*g), 0)),
                pl.BlockSpec(keys_pad.shape, lambda *g: (0, 0, 0))]
    out_specs = [pl.BlockSpec((tt, hk), lambda *g: (tile_of(*g), 0))] * 3
    out_shape = [jax.ShapeDtypeStruct((t, hk), I32), jax.ShapeDtypeStruct((t, hk), I32),
                 jax.ShapeDtypeStruct((t, hk), F32)]
    scratch = [pltpu.VMEM((heads, two_nkeys, tt), F32)] + [pltpu.VMEM((hk, tt), F32)] * 3
    return in_specs, out_specs, out_shape, scratch


def _route(pq, keys_pad, tt):
    heads, two_nkeys, _ = keys_pad.shape
    in_specs, out_specs, out_shape, scratch = _route_specs(pq, keys_pad, tt, lambda m, p: m)
    return pl.pallas_call(
        functools.partial(_route_kernel, heads=heads, nkeys=two_nkeys // 2),
        grid=(pq.shape[0] // tt, ROUTE_PARTS),
        in_specs=in_specs, out_specs=out_specs, out_shape=out_shape, scratch_shapes=scratch,
        compiler_params=_params(("parallel", "arbitrary")),
        name="peer_route",
    )(pq, keys_pad)


def _route_decode_kernel(pt_ref, *refs, heads, nkeys, a_dim, lam0, a_heads, n_route, n_decode, d_steps):
    del pt_ref
    s = pl.program_id(0)
    n_d_in = 8 + 2 * PAGES_PER_STEP
    d_in, r_in = refs[:n_d_in], refs[n_d_in:n_d_in + 2]
    outs = refs[n_d_in + 2:n_d_in + 6]
    d_scr, r_scr = refs[n_d_in + 6:n_d_in + 10], refs[n_d_in + 10:]

    def decode():
        return _decode_body(s % d_steps, d_steps, *d_in[:8], d_in[8:8 + PAGES_PER_STEP],
                            d_in[8 + PAGES_PER_STEP:], outs[0], *d_scr, a_dim=a_dim, lam0=lam0, heads=a_heads)

    def route():
        return _route_body(s % ROUTE_PARTS, ROUTE_PARTS, *r_in, *outs[1:], *r_scr, heads=heads, nkeys=nkeys)

    if n_route == n_decode:
        route_steps = (heads // ROUTE_PARTS) * 3 * P_TOPK
        _drain((decode(), 1), (route(), max(1, route_steps // (2 * PAGES_PER_STEP))))
    else:
        pl.when(s < n_decode)(lambda: _drain((decode(), 1)))
        pl.when(s < n_route)(lambda: _drain((route(), 1)))


def _route_and_decode(pq, keys_pad, tt, q, k_new, v_new, cache_k, cache_v, layer, page_table, subln, lams,
                      a_dim, lam0):
    bs, width = q.shape
    page, a_heads, hd = cache_k.shape[2:]
    n_pages = page_table.shape[1]
    d_steps = n_pages // PAGES_PER_STEP
    n_decode = bs * d_steps
    n_route = (pq.shape[0] // tt) * ROUTE_PARTS
    heads, two_nkeys, _ = keys_pad.shape
    pt = page_table.reshape(-1)
    q3, k3, v3 = (a.astype(F32).reshape(bs, a_heads, hd) for a in (q, k_new, v_new))
    sample = lambda s: jnp.minimum(s // d_steps, bs - 1)
    row = pl.BlockSpec((1, a_heads, hd), lambda s, pt: (sample(s), 0, 0))
    small = pl.BlockSpec((1, a_dim), lambda s, pt: (0, 0))

    def page_spec(i):
        def index(s, pt):
            sc = jnp.minimum(s, n_decode - 1)
            return (layer, pt[(sc // d_steps) * n_pages + (sc % d_steps) * PAGES_PER_STEP + i], 0, 0, 0)
        return pl.BlockSpec((1, 1, page, a_heads, hd), index)

    tile_of = lambda s, pt: jnp.minimum(s // ROUTE_PARTS, pq.shape[0] // tt - 1)
    r_in, r_out, r_shape, r_scr = _route_specs(pq, keys_pad, tt, tile_of)
    grid_spec = pltpu.PrefetchScalarGridSpec(
        num_scalar_prefetch=1,
        grid=(max(n_route, n_decode),),
        in_specs=[row, row, row, pl.BlockSpec((1, hd), lambda s, pt: (0, 0)), small, small, small, small]
        + [page_spec(i) for i in range(PAGES_PER_STEP)] * 2 + r_in,
        out_specs=[row] + r_out,
        scratch_shapes=[pltpu.VMEM((2 * a_heads, hd), BF16), pltpu.VMEM((2 * a_heads, 1), F32),
                        pltpu.VMEM((2 * a_heads, 1), F32), pltpu.VMEM((2 * a_heads, hd), F32)] + r_scr,
    )
    o, ii, jj, gate = pl.pallas_call(
        functools.partial(_route_decode_kernel, heads=heads, nkeys=two_nkeys // 2, a_dim=a_dim, lam0=lam0,
                          a_heads=a_heads, n_route=n_route, n_decode=n_decode, d_steps=d_steps),
        grid_spec=grid_spec,
        out_shape=[jax.ShapeDtypeStruct((bs, a_heads, hd), F32)] + r_shape,
        compiler_params=_params(("arbitrary",)),
        name="route_and_decode",
    )(pt, q3, k3, v3, subln, *lams, *([cache_k] * PAGES_PER_STEP), *([cache_v] * PAGES_PER_STEP),
      pq, keys_pad)
    return o.reshape(bs, width), ii, jj, gate


def _peer_act_kernel(hf_ref, u_ref, i_ref, j_ref, act_ref, *, nkeys):
    e = pl.program_id(1)

    @pl.when(e == 0)
    def _():
        act_ref[...] = jnp.zeros(act_ref.shape, F32)

    hf = hf_ref[...]
    ii = i_ref[...]
    jj = j_ref[...]
    act = act_ref[...]
    per = u_ref.shape[0] // nkeys
    for c0 in range(0, per, 2):
        s = _dot_nt(hf, u_ref[c0 * nkeys:(c0 + 2) * nkeys, :])
        for c in (c0, c0 + 1):
            picked = jnp.take_along_axis(s[:, (c - c0) * nkeys:(c - c0 + 1) * nkeys], jj, axis=1)
            act = jnp.where(ii == e * per + c, picked, act)
    act_ref[...] = act


def _peer_act(hf, u_bf16, ii, jj, nkeys, tt, eb):
    t, d = hf.shape
    hk = ii.shape[1]
    return pl.pallas_call(
        functools.partial(_peer_act_kernel, nkeys=nkeys),
        grid=(t // tt, u_bf16.shape[0] // eb),
        in_specs=[pl.BlockSpec((tt, d), lambda m, e: (m, 0)),
                  pl.BlockSpec((eb, d), lambda m, e: (e, 0)),
                  pl.BlockSpec((tt, hk), lambda m, e: (m, 0)),
                  pl.BlockSpec((tt, hk), lambda m, e: (m, 0))],
        out_specs=pl.BlockSpec((tt, hk), lambda m, e: (m, 0)),
        out_shape=jax.ShapeDtypeStruct((t, hk), F32),
        compiler_params=_params(("parallel", "arbitrary")),
        name="peer_act",
    )(hf, u_bf16, ii, jj)


def _peer_out_kernel(act_ref, g_ref, i_ref, j_ref, x_ref, v_ref, y_ref, wd_ref, w_ref, *, nkeys, pitch):
    e = pl.program_id(1)
    tt = act_ref.shape[0]
    hk = act_ref.shape[1]
    per = v_ref.shape[0] // nkeys
    half = nkeys // 2

    @pl.when(e == 0)
    def _():
        a = act_ref[...]
        gelu = 0.5 * a * (1.0 + lax.erf(a * (2.0 ** -0.5)))
        w_ref[...] = g_ref[...] * gelu
        y_ref[...] = x_ref[...]
        sub = lax.broadcasted_iota(I32, (nkeys, hk), 0)

        def group(gidx, carry):
            t0 = pl.multiple_of(gidx * SUBLANES, SUBLANES)
            w8 = w_ref[pl.ds(t0, SUBLANES), :]
            i8 = i_ref[pl.ds(t0, SUBLANES), :]
            j8 = j_ref[pl.ds(t0, SUBLANES), :]
            for r in range(SUBLANES):
                wi = jnp.where(sub == i8[r:r + 1], w8[r:r + 1], 0.0).astype(BF16)
                oj = jnp.where(sub == j8[r:r + 1], 1.0, 0.0).astype(BF16)
                tile = _dot_nt(wi, oj)
                lo = lax.bitcast_convert_type(tile[0:half].astype(BF16).astype(F32), U32) >> 16
                hi = lax.bitcast_convert_type(tile[half:nkeys].astype(BF16).astype(F32), U32) & HI16
                row0 = pl.multiple_of((t0 + r) * pitch, SUBLANES)
                wd_ref[pl.ds(row0, half), :] = lo | hi
            return carry

        lax.fori_loop(0, tt // SUBLANES, group, 0)

    low_half = e * per < half
    parts = []
    for c in range(per):
        w = wd_ref[pl.ds((e * per + c) % half, tt, stride=pitch), :]
        bits = jnp.where(low_half, w << 16, w & HI16)
        parts.append(lax.bitcast_convert_type(bits, F32).astype(BF16))
    y_ref[...] += _dot(jnp.concatenate(parts, axis=1), v_ref[...])


def _peer_out(act, gate, ii, jj, x2, v_bf16, nkeys, tt, eb):
    t, d = x2.shape
    hk = act.shape[1]
    assert (nkeys // 2) % (eb // nkeys) == 0
    pitch = nkeys // 2 + SUBLANES
    tok = lambda w: pl.BlockSpec((tt, w), lambda m, e: (m, 0))
    return pl.pallas_call(
        functools.partial(_peer_out_kernel, nkeys=nkeys, pitch=pitch),
        grid=(t // tt, v_bf16.shape[0] // eb),
        in_specs=[tok(hk), tok(hk), tok(hk), tok(hk), tok(d),
                  pl.BlockSpec((eb, d), lambda m, e: (e, 0))],
        out_specs=tok(d),
        out_shape=jax.ShapeDtypeStruct((t, d), F32),
        scratch_shapes=[pltpu.VMEM((tt * pitch, nkeys), U32), pltpu.VMEM((tt, hk), F32)],
        compiler_params=_params(("parallel", "arbitrary")),
        name="peer_out",
    )(act, gate, ii, jj, x2, v_bf16)


def _tile(n, pref):
    return pref if n % pref == 0 else n


def _layer_consts(lw, a_dim, heads_s, head_dim, inner, conv_dim):
    d = lw['w_in'].shape[0]
    a_width = lw['w_out'].shape[0] - inner
    tn = a_width
    cols = lw['w_in'].shape[1]
    assert cols - 4 * tn - conv_dim == heads_s <= LANES
    n_pad = 4 * tn + -(-(conv_dim + LANES) // tn) * tn
    w_pad = jnp.zeros((d, n_pad), BF16).at[:, :cols].set(lw['w_in'].astype(BF16))
    reps = a_width // a_dim
    colgain = jnp.concatenate([jnp.tile(lw['q_norm'], reps) * (a_dim ** -0.5),
                               jnp.tile(lw['k_norm'], reps)])[None, :]
    gid = jnp.arange(LANES) // a_dim
    bd = (gid[:, None] == gid[None, :]).astype(BF16)
    pad_h = lambda v: jnp.zeros((1, LANES), F32).at[0, :heads_s].set(v)
    tri = (jnp.arange(S_CHUNK)[None, :] <= jnp.arange(S_CHUNK)[:, None]).astype(BF16)
    expand = (jnp.arange(LANES)[:, None] == (jnp.arange(inner) // head_dim)[None, :]).astype(BF16)
    keys = lw['peer_keys']
    _, p_heads, nkeys, half = keys.shape
    keys_pad = jnp.zeros((p_heads, 2 * nkeys, 2 * half), BF16)
    keys_pad = keys_pad.at[:, :nkeys, :half].set(keys[0].astype(BF16))
    keys_pad = keys_pad.at[:, nkeys:, half:].set(keys[1].astype(BF16))
    return dict(
        w_pad=w_pad, colgain=colgain, bd=bd, tri=tri, expand=expand, keys_pad=keys_pad,
        norm_mix=lw['norm_mix'][None, :], subln=lw['subln'][None, :],
        lams=tuple(lw[n][None, :] for n in ('lambda_q1', 'lambda_k1', 'lambda_q2', 'lambda_k2')),
        conv_w=lw['conv_w'], conv_b=lw['conv_b'][None, :],
        dtb=pad_h(lw['dt_bias']), alog=pad_h(lw['a_log']),
        dskip=jnp.repeat(lw['d_skip'], head_dim)[None, :], ssm_norm=lw['ssm_norm'][None, :],
        w_out=lw['w_out'].astype(BF16), norm_x=lw['norm_x'][None, :], wq_x=lw['wq_x'].astype(BF16),
        qx_norm=lw['qx_norm'][None, :], kx_norm=lw['kx_norm'][None, :], norm_mem=lw['norm_mem'][None, :],
        w_kv=jnp.concatenate([lw['wk_x'], lw['wv_x']], axis=1).astype(BF16),
        wo_x=lw['wo_x'].astype(BF16), norm_ffn=lw['norm_ffn'][None, :],
        peer_wq=lw['peer_wq'].astype(BF16), peer_u=lw['peer_u'].astype(BF16),
        peer_v=lw['peer_v'].astype(BF16), nkeys=nkeys,
    )


def _peer(c, x2, hf, routing, tt_act, tt_out):
    t = x2.shape[0]
    nkeys = c['nkeys']
    ii, jj, gate = routing
    act = _peer_act(hf, c['peer_u'], ii, jj, nkeys, _tile(t, tt_act), PEER_ACT_BLOCKS * nkeys)
    return _peer_out(act, gate, ii, jj, x2, c['peer_v'], nkeys, _tile(t, tt_out), PEER_OUT_BLOCKS * nkeys)


def kernel(x_prompt, x_sample, cache_k, cache_v, cache_mem_k, cache_mem_v, state_conv, state_ssm, page_table, mem_prompt, norm_mix, w_in, q_norm, k_norm, lambda_q1, lambda_k1, lambda_q2, lambda_k2, subln, conv_w, conv_b, dt_bias, a_log, d_skip, ssm_norm, w_out, norm_x, norm_mem, wq_x, wk_x, wv_x, qx_norm, kx_norm, wo_x, norm_ffn, peer_wq, peer_keys, peer_u, peer_v):
    weights = dict(norm_mix=norm_mix, w_in=w_in, q_norm=q_norm, k_norm=k_norm, lambda_q1=lambda_q1,
                   lambda_k1=lambda_k1, lambda_q2=lambda_q2, lambda_k2=lambda_k2, subln=subln,
                   conv_w=conv_w, conv_b=conv_b, dt_bias=dt_bias, a_log=a_log, d_skip=d_skip,
                   ssm_norm=ssm_norm, w_out=w_out, norm_x=norm_x, norm_mem=norm_mem, wq_x=wq_x,
                   wk_x=wk_x, wv_x=wv_x, qx_norm=qx_norm, kx_norm=kx_norm, wo_x=wo_x,
                   norm_ffn=norm_ffn, peer_wq=peer_wq, peer_keys=peer_keys, peer_u=peer_u, peer_v=peer_v)
    depth = w_in.shape[0]
    bp, seq, d = x_prompt.shape
    bs, dec_seq, _ = x_sample.shape
    assert dec_seq == 1
    a_dim = q_norm.shape[-1]
    a_heads = cache_k.shape[3]
    a_width = a_heads * 2 * a_dim
    s_heads, head_dim, state = state_ssm.shape[2:]
    inner = s_heads * head_dim
    conv_dim = state_conv.shape[-1]
    n_mem, x_heads, x_dim = cache_mem_k.shape[2:]
    x_width = x_heads * x_dim
    assert a_width == inner and 2 * a_dim == LANES and 2 * head_dim == LANES and state == LANES

    yp = x_prompt.reshape(bp * seq, d)
    ys = x_sample.reshape(bs, d)
    outs = {n: [] for n in ('kp', 'vp', 'cp', 'sp', 'mkp', 'mvp', 'ks', 'vs', 'cs', 'ss')}
    for l in range(depth):
        lw = {n: w[l] for n, w in weights.items()}
        c = _layer_consts(lw, a_dim, s_heads, head_dim, inner, conv_dim)
        lam0 = 0.8 - 0.6 * math.exp(-0.3 * l)

        mk, mv = _memkv(mem_prompt.reshape(bp * n_mem, d), c['norm_mem'], c['w_kv'], c['kx_norm'],
                        _tile(bp * n_mem, 256))
        q, k, v, z, xd = _inproj(yp, c['norm_mix'], c['w_pad'], c['colgain'], c['bd'], a_dim,
                                 _tile(bp * seq, 512))
        o = _attn_prompt(q, k, v, c['subln'].T, c['lams'], bp, seq, a_dim, lam0, _tile(seq, 256))
        ym, st = _ssd_prompt(xd, z, c['conv_w'], c['conv_b'], c['dtb'], c['alog'], c['dskip'],
                             c['ssm_norm'], c['tri'], bp, seq, inner, state, conv_dim, head_dim)
        x1, qx = _outproj(yp, o, ym, c['w_out'], c['norm_x'], c['wq_x'], c['qx_norm'], _tile(bp * seq, 256))
        ox = _xattn_prompt(qx, mk, mv, bp, seq, n_mem, x_dim, _tile(seq, 256))
        x2, hf, pq = _xproj(x1, ox, c['wo_x'], c['norm_ffn'], c['peer_wq'], _tile(bp * seq, 256))
        qs, ks, vs, zs, xds = _inproj(ys, c['norm_mix'], c['w_pad'], c['colgain'], c['bd'], a_dim, bs)
        o_s, *routing = _route_and_decode(pq, c['keys_pad'], LANES, qs, ks, vs, cache_k, cache_v, l,
                                          page_table, c['subln'], c['lams'], a_dim, lam0)
        yp = _peer(c, x2, hf, routing, 512, 512)
        outs['kp'].append(k.reshape(bp, seq, a_heads, 2 * a_dim))
        outs['vp'].append(v.reshape(bp, seq, a_heads, 2 * a_dim))
        outs['cp'].append(xd.reshape(bp, seq, -1)[:, seq - (conv_w.shape[1] - 1):, :conv_dim])
        outs['sp'].append(st.reshape(bp, s_heads, head_dim, state))
        outs['mkp'].append(mk.reshape(bp, n_mem, x_heads, x_dim))
        outs['mvp'].append(mv.reshape(bp, n_mem, x_heads, x_dim))

        k, v, z, xd = ks, vs, zs, xds
        o = o_s.astype(BF16)
        conv_prev = state_conv[l]
        xbc, xdt_t, da = _ssd_step_prep(xd, jnp.swapaxes(conv_prev, 0, 1), c['conv_w'], c['conv_b'],
                                        c['dtb'], c['alog'], c['expand'], inner, conv_dim)
        hn, yt = _ssd_step(da[:, :s_heads].reshape(-1), state_ssm.reshape(depth * bs, inner, state), l,
                           xdt_t, xbc[:, inner:], state, head_dim)
        ym = _ssd_step_out(yt, xbc, z, c['dskip'], c['ssm_norm'], inner)
        x1, qx = _outproj(ys, o, ym, c['w_out'], c['norm_x'], c['wq_x'], c['qx_norm'], bs)
        ox = _xattn_decode(qx, cache_mem_k, cache_mem_v, l, 4)
        x2, hf, pq = _xproj(x1, ox, c['wo_x'], c['norm_ffn'], c['peer_wq'], bs)
        ys = _peer(c, x2, hf, _route(pq, c['keys_pad'], LANES), bs, bs)
        outs['ks'].append(k.reshape(bs, 1, a_heads, 2 * a_dim))
        outs['vs'].append(v.reshape(bs, 1, a_heads, 2 * a_dim))
        outs['cs'].append(jnp.concatenate([conv_prev[:, 1:], xd[:, None, :conv_dim]], axis=1))
        outs['ss'].append(hn.reshape(bs, s_heads, head_dim, state))

    st = lambda n: jnp.stack(outs[n]) if depth > 1 else outs[n][0][None]
    return (yp.reshape(bp, seq, d), ys.reshape(bs, 1, d), st('kp'), st('vp'), st('cp'), st('sp'),
            st('mkp'), st('mvp'), st('ks'), st('vs'), st('cs'), st('ss'))
```
